```python
import jax, jax.numpy as jnp
from jax import lax
import numpy as np

D_MODEL = 2048
BATCH = 8
SEQ = 4096
DEPTH = 1
DEC_BATCH = 32
DEC_SEQ = 32
PAST_LEN = 1024

CHUNK = 64
N_FOX_HEADS = 8
FOX_HEAD_DIM = 128
FOX_WIDTH = N_FOX_HEADS * FOX_HEAD_DIM
FOX_SCALE = FOX_HEAD_DIM ** -0.5
N_SGU_GROUPS = 8
SGU_GROUP_DIM = 128
SGU_WIDTH = N_SGU_GROUPS * SGU_GROUP_DIM
SGU_CHUNK = 128
MIX_WIDTH = FOX_WIDTH + SGU_WIDTH
IN_WIDTH = 3 * FOX_WIDTH + N_FOX_HEADS + 2 * SGU_WIDTH
Q_BLOCK = 128
PEER_HEADS = 8
PEER_N_KEYS = 128
PEER_N_EXPERTS = PEER_N_KEYS * PEER_N_KEYS
PEER_QUERY_DIM = 256
PEER_HALF = PEER_QUERY_DIM // 2
PEER_TOPK = 16
PEER_TOKEN_BLOCK = 128
EPS = 1e-6
NEG_INF = -1e30

kernel_name = 'fox_sgu_peer_streaming_step'


def rms_norm(x, g):
    xf = x.astype(jnp.float32)
    y = xf * lax.rsqrt(jnp.mean(xf * xf, axis=-1, keepdims=True) + EPS)
    return (y * g.astype(jnp.float32)).astype(x.dtype)


def mixer_inputs(h, w_in, b_f, q_g, k_g, sgu_v_g):
    B, T = h.shape[:2]
    z = jnp.einsum('btd,de->bte', h, w_in)
    s1 = FOX_WIDTH
    s2 = 2 * FOX_WIDTH
    s3 = 3 * FOX_WIDTH
    s4 = s3 + N_FOX_HEADS
    s5 = s4 + SGU_WIDTH
    q, k, v, f_logit, u_s, v_s = jnp.split(z, [s1, s2, s3, s4, s5], axis=-1)
    q = rms_norm(q.reshape(B, T, N_FOX_HEADS, FOX_HEAD_DIM), q_g)
    k = rms_norm(k.reshape(B, T, N_FOX_HEADS, FOX_HEAD_DIM), k_g)
    v = v.reshape(B, T, N_FOX_HEADS, FOX_HEAD_DIM)
    logf = jax.nn.log_sigmoid((f_logit + b_f).astype(jnp.float32))
    u_s = jax.nn.gelu(u_s).reshape(B, T, N_SGU_GROUPS, SGU_GROUP_DIM)
    v_s = rms_norm(jax.nn.gelu(v_s).reshape(B, T, N_SGU_GROUPS, SGU_GROUP_DIM), sgu_v_g)
    return q, k, v, logf, u_s, v_s


def fox_attend(q, cq, q_pos, k, v, ck, k_pos):
    s = jnp.einsum('bqhd,bkhd->bhqk', q, k).astype(jnp.float32) * FOX_SCALE
    decay = cq.transpose(0, 2, 1)[:, :, :, None] - ck.transpose(0, 2, 1)[:, :, None, :]
    mask = k_pos[None, :] <= q_pos[:, None]
    p = jax.nn.softmax(jnp.where(mask, s + decay, NEG_INF), axis=-1)
    return jnp.einsum('bhqk,bkhd->bqhd', p.astype(v.dtype), v)


def fox_prompt(q, k, v, cum):
    B, T, H, Dh = q.shape
    n = T // Q_BLOCK
    qb = q.reshape(B, n, Q_BLOCK, H, Dh).swapaxes(0, 1)
    cb = cum.reshape(B, n, Q_BLOCK, H).swapaxes(0, 1)
    k_pos = jnp.arange(T)

    def block(args):
        i, q_i, c_i = args
        return fox_attend(q_i, c_i, i * Q_BLOCK + jnp.arange(Q_BLOCK), k, v, cum, k_pos)

    out = lax.map(block, (jnp.arange(n), qb, cb))
    return out.swapaxes(0, 1).reshape(B, T, H, Dh)


def sgu_mix(u, v, w_s, b_s):
    L = v.shape[2]
    tril = jnp.tril(jnp.ones((L, L), dtype=v.dtype))
    w = w_s[:, :L, :L] * tril
    mixed = jnp.einsum('gts,bnsgd->bntgd', w, v) + b_s[:, :L].T[None, None, :, :, None]
    return u * mixed


def merge_out(fox, sgu, g_fox, g_sgu, w_out):
    B, T = fox.shape[:2]
    cat = jnp.concatenate([rms_norm(fox, g_fox).reshape(B, T, FOX_WIDTH),
                           rms_norm(sgu, g_sgu).reshape(B, T, SGU_WIDTH)], axis=-1)
    return jnp.einsum('btm,md->btd', cat, w_out)


def peer_ffn(h, w_q, sub_keys, u_tab, v_tab):
    B, T, D = h.shape
    n = B * T
    nb = -(-n // PEER_TOKEN_BLOCK)
    x = jnp.pad(h.reshape(n, D), ((0, nb * PEER_TOKEN_BLOCK - n), (0, 0)))
    x = x.reshape(nb, PEER_TOKEN_BLOCK, D)

    def block(xb):
        q = jnp.einsum('td,dq->tq', xb, w_q).reshape(-1, PEER_HEADS, 2, PEER_HALF)
        s = jnp.einsum('thcd,hckd->thck', q, sub_keys).astype(jnp.float32)
        sv, si = lax.top_k(s, PEER_TOPK)
        cand = sv[:, :, 0, :, None] + sv[:, :, 1, None, :]
        cand_idx = si[:, :, 0, :, None] * PEER_N_KEYS + si[:, :, 1, None, :]
        t = xb.shape[0]
        top_s, top_pos = lax.top_k(cand.reshape(t, PEER_HEADS, PEER_TOPK * PEER_TOPK), PEER_TOPK)
        idx = jnp.take_along_axis(cand_idx.reshape(t, PEER_HEADS, PEER_TOPK * PEER_TOPK), top_pos, axis=-1)
        g = jax.nn.softmax(top_s, axis=-1)
        act = jax.nn.gelu(jnp.einsum('td,thkd->thk', xb, u_tab[idx]).astype(jnp.float32)) * g
        return jnp.einsum('thk,thkd->td', act.astype(xb.dtype), v_tab[idx])

    y = lax.map(block, x).reshape(nb * PEER_TOKEN_BLOCK, D)[:n]
    return y.reshape(B, T, D)


def setup_inputs(seed: int = 0) -> dict:
    key = jax.random.key(seed)
    ks = jax.random.split(key, 24)

    def nrm(k, shape, scale):
        return jax.random.normal(k, shape, jnp.float32) * scale

    Hd = (N_FOX_HEADS, FOX_HEAD_DIM)
    Gd = (N_SGU_GROUPS, SGU_GROUP_DIM)
    return {
        'x_prompt': nrm(ks[0], (BATCH, SEQ, D_MODEL), 1.0),
        'x_sample': nrm(ks[1], (DEC_BATCH, DEC_SEQ, D_MODEL), 1.0),
        'cache_fox_k': nrm(ks[2], (DEPTH, DEC_BATCH, PAST_LEN) + Hd, 1.0),
        'cache_fox_v': nrm(ks[3], (DEPTH, DEC_BATCH, PAST_LEN) + Hd, 1.0),
        'cache_fox_logf': jax.nn.log_sigmoid(3.0 + nrm(ks[4], (DEPTH, DEC_BATCH, PAST_LEN, N_FOX_HEADS), 1.0)),
        'norm_mix_g': 1.0 + nrm(ks[5], (DEPTH, D_MODEL), 0.02),
        'w_in': nrm(ks[6], (DEPTH, D_MODEL, IN_WIDTH), D_MODEL ** -0.5),
        'b_f': 3.0 + nrm(ks[7], (DEPTH, N_FOX_HEADS), 0.1),
        'q_norm_g': 1.0 + nrm(ks[8], (DEPTH, FOX_HEAD_DIM), 0.02),
        'k_norm_g': 1.0 + nrm(ks[9], (DEPTH, FOX_HEAD_DIM), 0.02),
        'sgu_v_norm_g': 1.0 + nrm(ks[10], (DEPTH,) + Gd, 0.02),
        'w_s': nrm(ks[11], (DEPTH, N_SGU_GROUPS, SGU_CHUNK, SGU_CHUNK), SGU_CHUNK ** -0.5),
        'b_s': 1.0 + nrm(ks[12], (DEPTH, N_SGU_GROUPS, SGU_CHUNK), 0.1),
        'fox_out_norm_g': 1.0 + nrm(ks[13], (DEPTH,) + Hd, 0.02),
        'sgu_out_norm_g': 1.0 + nrm(ks[14], (DEPTH,) + Gd, 0.02),
        'w_out': nrm(ks[15], (DEPTH, MIX_WIDTH, D_MODEL), MIX_WIDTH ** -0.5),
        'norm_ffn_g': 1.0 + nrm(ks[16], (DEPTH, D_MODEL), 0.02),
        'peer_w_q': nrm(ks[17], (DEPTH, D_MODEL, PEER_HEADS * PEER_QUERY_DIM), D_MODEL ** -0.5),
        'peer_sub_keys': nrm(ks[18], (DEPTH, PEER_HEADS, 2, PEER_N_KEYS, PEER_HALF), PEER_HALF ** -0.5),
        'peer_u': nrm(ks[19], (DEPTH, PEER_N_EXPERTS, D_MODEL), D_MODEL ** -0.5),
        'peer_v': nrm(ks[20], (DEPTH, PEER_N_EXPERTS, D_MODEL), PEER_HEADS ** -0.5),
    }


def reference(x_prompt, x_sample, cache_fox_k, cache_fox_v, cache_fox_logf, norm_mix_g, w_in, b_f,
              q_norm_g, k_norm_g, sgu_v_norm_g, w_s, b_s, fox_out_norm_g, sgu_out_norm_g, w_out,
              norm_ffn_g, peer_w_q, peer_sub_keys, peer_u, peer_v):
    y_p = x_prompt
    y_s = x_sample
    P = cache_fox_k.shape[2]
    Ts = x_sample.shape[1]
    kp, vp, fp, ksl, vsl, fsl, gsl = [], [], [], [], [], [], []
    for l in range(DEPTH):
        B, T = y_p.shape[:2]
        h = rms_norm(y_p, norm_mix_g[l])
        q, k, v, logf, u_g, v_g = mixer_inputs(h, w_in[l], b_f[l], q_norm_g[l], k_norm_g[l], sgu_v_norm_g[l])
        fox = fox_prompt(q, k, v, jnp.cumsum(logf, axis=1))
        nc = T // SGU_CHUNK
        sgu = sgu_mix(u_g.reshape(B, nc, SGU_CHUNK, N_SGU_GROUPS, SGU_GROUP_DIM),
                      v_g.reshape(B, nc, SGU_CHUNK, N_SGU_GROUPS, SGU_GROUP_DIM),
                      w_s[l], b_s[l]).reshape(B, T, N_SGU_GROUPS, SGU_GROUP_DIM)
        y_p = y_p + merge_out(fox, sgu, fox_out_norm_g[l], sgu_out_norm_g[l], w_out[l])
        y_p = y_p + peer_ffn(rms_norm(y_p, norm_ffn_g[l]), peer_w_q[l], peer_sub_keys[l], peer_u[l], peer_v[l])
        kp.append(k)
        vp.append(v)
        fp.append(logf)

        h = rms_norm(y_s, norm_mix_g[l])
        q, k, v, logf, u_g, v_g = mixer_inputs(h, w_in[l], b_f[l], q_norm_g[l], k_norm_g[l], sgu_v_norm_g[l])
        k_all = jnp.concatenate([cache_fox_k[l].astype(k.dtype), k], axis=1)
        v_all = jnp.concatenate([cache_fox_v[l].astype(v.dtype), v], axis=1)
        cum_all = jnp.cumsum(jnp.concatenate([cache_fox_logf[l].astype(jnp.float32), logf], axis=1), axis=1)
        k_pos = jnp.arange(P + Ts)
        fox = fox_attend(q, cum_all[:, P:], k_pos[P:], k_all, v_all, cum_all, k_pos)
        sgu = sgu_mix(u_g[:, None], v_g[:, None], w_s[l], b_s[l])[:, 0]
        y_s = y_s + merge_out(fox, sgu, fox_out_norm_g[l], sgu_out_norm_g[l], w_out[l])
        y_s = y_s + peer_ffn(rms_norm(y_s, norm_ffn_g[l]), peer_w_q[l], peer_sub_keys[l], peer_u[l], peer_v[l])
        ksl.append(k)
        vsl.append(v)
        fsl.append(logf)
        gsl.append(v_g)
    return (y_p, y_s, jnp.stack(kp), jnp.stack(vp), jnp.stack(fp), jnp.stack(ksl), jnp.stack(vsl), jnp.stack(fsl), jnp.stack(gsl))
```

```python
import functools

import jax
import jax.numpy as jnp
from jax import lax
from jax.experimental import pallas as pl
from jax.experimental.pallas import tpu as pltpu

EPS = 1e-6
NEG_INF = -1e30
LANES = 128
SUBLANES = 8
PEER_TOPK = 16
SGU_CHUNK = 128
VMEM_LIMIT = 52 * 1024 * 1024

F32 = jnp.float32
BF16 = jnp.bfloat16


def _params(sem):
    return pltpu.CompilerParams(dimension_semantics=sem, vmem_limit_bytes=VMEM_LIMIT)


def _rms(x, g):
    return x * lax.rsqrt(jnp.mean(x * x, axis=-1, keepdims=True) + EPS) * g


def _log_sigmoid(x):
    return jnp.minimum(x, 0.0) - jnp.log1p(jnp.exp(-jnp.abs(x)))


def _inproj_kernel(x_ref, g_ref, w_ref, wf_ref, bf_ref, qg_ref, kg_ref, vg_ref,
                   q_ref, k_ref, v_ref, us_ref, vs_ref, lf_ref, h_scr):
    s = pl.program_id(1)
    width = w_ref.shape[1]

    @pl.when(s == 0)
    def _():
        hb = _rms(x_ref[...], g_ref[...]).astype(BF16)
        h_scr[...] = hb
        fl = lax.dot_general(wf_ref[...], hb, (((1,), (1,)), ((), ())), preferred_element_type=F32)
        lf_ref[...] = _log_sigmoid(fl + bf_ref[...])

    z = jnp.dot(h_scr[...], w_ref[...], preferred_element_type=F32)

    def headwise(o_ref, fn):
        for h in range(width // LANES):
            sl = slice(h * LANES, (h + 1) * LANES)
            o_ref[:, sl] = fn(z[:, sl], sl)

    @pl.when(s == 0)
    def _():
        headwise(q_ref, lambda zh, sl: _rms(zh, qg_ref[:, sl]))

    @pl.when(s == 1)
    def _():
        headwise(k_ref, lambda zh, sl: _rms(zh, kg_ref[:, sl]))

    @pl.when(s == 2)
    def _():
        v_ref[...] = z

    @pl.when(s == 3)
    def _():
        us_ref[...] = jax.nn.gelu(z)

    @pl.when(s == 4)
    def _():
        headwise(vs_ref, lambda zh, sl: _rms(jax.nn.gelu(zh), vg_ref[:, sl]))


def _inproj(x2d, norm_g, w_main, w_f_t, b_f, q_g, k_g, vs_g, tm):
    n, d = x2d.shape
    width = w_main.shape[1] // 5
    nh = w_f_t.shape[0]
    full = lambda shape: pl.BlockSpec(shape, lambda i, s: (0,) * len(shape))
    tok = pl.BlockSpec((tm, width), lambda i, s: (i, 0))
    out_sds = jax.ShapeDtypeStruct((n, width), F32)
    return pl.pallas_call(
        _inproj_kernel,
        grid=(n // tm, 5),
        in_specs=[pl.BlockSpec((tm, d), lambda i, s: (i, 0)), full((1, d)),
                  pl.BlockSpec((d, width), lambda i, s: (0, s)), full((nh, d)), full((nh, 1)),
                  full((1, width)), full((1, width)), full((1, width))],
        out_specs=[tok, tok, tok, tok, tok, pl.BlockSpec((nh, tm), lambda i, s: (0, i))],
        out_shape=[out_sds] * 5 + [jax.ShapeDtypeStruct((nh, n), F32)],
        scratch_shapes=[pltpu.VMEM((tm, d), BF16)],
        compiler_params=_params(("parallel", "arbitrary")),
        name="inproj",
    )(x2d, norm_g, w_main, w_f_t, b_f, q_g, k_g, vs_g)


def _cumsum_kernel(x_ref, o_ref):
    rows, length = x_ref.shape
    r = lax.broadcasted_iota(jnp.int32, (LANES, LANES), 0)
    c = lax.broadcasted_iota(jnp.int32, (LANES, LANES), 1)
    tri = jnp.where(r <= c, 1.0, 0.0).astype(BF16)
    carry = jnp.zeros((rows, 1), F32)
    for j in range(length // LANES):
        sl = slice(j * LANES, (j + 1) * LANES)
        x = x_ref[:, sl]
        hi = x.astype(BF16)
        r1 = x - hi.astype(F32)
        mid = r1.astype(BF16)
        lo = (r1 - mid.astype(F32)).astype(BF16)
        cs = (jnp.dot(hi, tri, preferred_element_type=F32)
              + jnp.dot(mid, tri, preferred_element_type=F32)
              + jnp.dot(lo, tri, preferred_element_type=F32))
        out = cs + carry
        o_ref[:, sl] = out
        carry = out[:, LANES - 1:LANES]


def _cumsum_rows(x):
    rows, length = x.shape
    rb = _pick_tile(rows, 8)
    return pl.pallas_call(
        _cumsum_kernel,
        grid=(rows // rb,),
        in_specs=[pl.BlockSpec((rb, length), lambda i: (i, 0))],
        out_specs=pl.BlockSpec((rb, length), lambda i: (i, 0)),
        out_shape=jax.ShapeDtypeStruct((rows, length), F32),
        compiler_params=_params(("parallel",)),
        name="cumsum",
    )(x)


def _attn_prompt_kernel(q_ref, k_ref, v_ref, cq_ref, ck_ref, g_ref, o_ref, m_scr, l_scr, acc_scr, *, scale):
    h = pl.program_id(1)
    qi = pl.program_id(2)
    ki = pl.program_id(3)
    tq = q_ref.shape[0]
    tk = k_ref.shape[0]

    @pl.when(ki == 0)
    def _():
        m_scr[...] = jnp.full(m_scr.shape, NEG_INF, F32)
        l_scr[...] = jnp.zeros(l_scr.shape, F32)
        acc_scr[...] = jnp.zeros(acc_scr.shape, F32)

    def step(masked):
        q = q_ref[...].astype(BF16)
        k = k_ref[...].astype(BF16)
        s = lax.dot_general(q, k, (((1,), (1,)), ((), ())), preferred_element_type=F32) * scale
        lane = lax.broadcasted_iota(jnp.int32, cq_ref.shape, 1)
        cq = jnp.sum(jnp.where(lane == h, cq_ref[...], 0.0), axis=-1, keepdims=True)
        s = s + (cq - ck_ref[0])
        if masked:
            qpos = lax.broadcasted_iota(jnp.int32, (tq, tk), 0)
            kpos = lax.broadcasted_iota(jnp.int32, (tq, tk), 1)
            s = jnp.where(kpos <= qpos, s, NEG_INF)
        m_old = m_scr[...]
        m_new = jnp.maximum(m_old, jnp.max(s, axis=-1, keepdims=True))
        p = jnp.exp(s - m_new)
        alpha = jnp.exp(m_old - m_new)
        l_scr[...] = alpha * l_scr[...] + jnp.sum(p, axis=-1, keepdims=True)
        acc_scr[...] = alpha * acc_scr[...] + jnp.dot(p.astype(BF16), v_ref[...].astype(BF16),
                                                      preferred_element_type=F32)
        m_scr[...] = m_new

    @pl.when(ki < qi)
    def _():
        step(False)

    @pl.when(ki == qi)
    def _():
        step(True)
        o = acc_scr[...] / l_scr[...]
        o_ref[...] = _rms(o, g_ref[0])


def _attn_prompt(q, k, v, cq_col, ck_row, g_fox, batch, seq, tq):
    n, width = q.shape
    nh = width // LANES
    nt = seq // tq
    scale = float(LANES) ** -0.5

    def q_map(b, h, qi, ki):
        return (b * nt + qi, h)

    def kv_map(b, h, qi, ki):
        return (b * nt + jnp.minimum(ki, qi), h)

    return pl.pallas_call(
        functools.partial(_attn_prompt_kernel, scale=scale),
        grid=(batch, nh, nt, nt),
        in_specs=[pl.BlockSpec((tq, LANES), q_map),
                  pl.BlockSpec((tq, LANES), kv_map),
                  pl.BlockSpec((tq, LANES), kv_map),
                  pl.BlockSpec((tq, nh), lambda b, h, qi, ki: (b * nt + qi, 0)),
                  pl.BlockSpec((1, 1, tq), lambda b, h, qi, ki: (b * nh + h, 0, jnp.minimum(ki, qi))),
                  pl.BlockSpec((1, 1, LANES), lambda b, h, qi, ki: (h, 0, 0))],
        out_specs=pl.BlockSpec((tq, LANES), q_map),
        out_shape=jax.ShapeDtypeStruct((n, width), F32),
        scratch_shapes=[pltpu.VMEM((tq, 1), F32), pltpu.VMEM((tq, 1), F32), pltpu.VMEM((tq, LANES), F32)],
        compiler_params=_params(("parallel", "parallel", "parallel", "arbitrary")),
        name="attn_prompt",
    )(q, k, v, cq_col, ck_row, g_fox)


def _attn_sample_kernel(q_ref, kn_ref, vn_ref, kc_ref, vc_ref, cq_ref, ckc_ref, ckn_ref, g_ref, o_ref, *, scale):
    ts = q_ref.shape[0]
    nh = q_ref.shape[1] // LANES
    dn = (((1,), (1,)), ((), ()))
    qpos = lax.broadcasted_iota(jnp.int32, (ts, ts), 0)
    kpos = lax.broadcasted_iota(jnp.int32, (ts, ts), 1)
    for h in range(nh):
        sl = slice(h * LANES, (h + 1) * LANES)
        q = q_ref[:, sl].astype(BF16)
        cq = cq_ref[:, h:h + 1]
        s1 = lax.dot_general(q, kc_ref[0, :, sl].astype(BF16), dn, preferred_element_type=F32) * scale
        s1 = s1 + (cq - ckc_ref[0, h:h + 1, :])
        s2 = lax.dot_general(q, kn_ref[:, sl].astype(BF16), dn, preferred_element_type=F32) * scale
        s2 = jnp.where(kpos <= qpos, s2 + (cq - ckn_ref[0, h:h + 1, :]), NEG_INF)
        m = jnp.maximum(jnp.max(s1, axis=-1, keepdims=True), jnp.max(s2, axis=-1, keepdims=True))
        p1 = jnp.exp(s1 - m)
        p2 = jnp.exp(s2 - m)
        l = jnp.sum(p1, axis=-1, keepdims=True) + jnp.sum(p2, axis=-1, keepdims=True)
        o = (jnp.dot(p1.astype(BF16), vc_ref[0, :, sl].astype(BF16), preferred_element_type=F32)
             + jnp.dot(p2.astype(BF16), vn_ref[:, sl].astype(BF16), preferred_element_type=F32)) / l
        o_ref[:, sl] = _rms(o, g_ref[:, sl])


def _attn_sample(q, k_new, v_new, k_cache, v_cache, cq_col, ck_cache, ck_new, g_fox_row, ts):
    n, width = q.shape
    nb, past, _ = k_cache.shape
    nh = width // LANES
    scale = float(LANES) ** -0.5
    tok = pl.BlockSpec((ts, width), lambda b: (b, 0))
    cache = pl.BlockSpec((1, past, width), lambda b: (b, 0, 0))
    return pl.pallas_call(
        functools.partial(_attn_sample_kernel, scale=scale),
        grid=(nb,),
        in_specs=[tok, tok, tok, cache, cache,
                  pl.BlockSpec((ts, nh), lambda b: (b, 0)),
                  pl.BlockSpec((1, nh, past), lambda b: (b, 0, 0)),
                  pl.BlockSpec((1, nh, ts), lambda b: (b, 0, 0)),
                  pl.BlockSpec((1, width), lambda b: (0, 0))],
        out_specs=tok,
        out_shape=jax.ShapeDtypeStruct((n, width), F32),
        compiler_params=_params(("parallel",)),
        name="attn_sample",
    )(q, k_new, v_new, k_cache, v_cache, cq_col, ck_cache, ck_new, g_fox_row)


def _merge_kernel(x_ref, fox_ref, us_ref, vs_ref, ws_ref, bs_ref, gs_ref, wo1_ref, wo2_ref, o_ref, cat_scr):
    tm = x_ref.shape[0]
    ng, chunk, _ = ws_ref.shape
    r = lax.broadcasted_iota(jnp.int32, (chunk, chunk), 0)
    c = lax.broadcasted_iota(jnp.int32, (chunk, chunk), 1)
    for g in range(ng):
        sl = slice(g * LANES, (g + 1) * LANES)
        wg = jnp.where(c <= r, ws_ref[g], 0.0).astype(BF16)
        bias = bs_ref[g]
        for j in range(tm // chunk):
            rows = slice(j * chunk, (j + 1) * chunk)
            mixed = jnp.dot(wg, vs_ref[rows, sl].astype(BF16), preferred_element_type=F32) + bias
            cat_scr[rows, sl] = _rms(us_ref[rows, sl] * mixed, gs_ref[:, sl]).astype(BF16)
    o_ref[...] = (x_ref[...]
                  + jnp.dot(fox_ref[...].astype(BF16), wo1_ref[...], preferred_element_type=F32)
                  + jnp.dot(cat_scr[...], wo2_ref[...], preferred_element_type=F32))


def _merge(x2d, fox_n, us, vs, w_s, b_s_col, g_sgu_row, wo1, wo2, tm):
    n, d = x2d.shape
    width = fox_n.shape[1]
    ng, chunk, _ = w_s.shape
    full = lambda shape: pl.BlockSpec(shape, lambda i: (0,) * len(shape))
    tok = pl.BlockSpec((tm, width), lambda i: (i, 0))
    row = pl.BlockSpec((tm, d), lambda i: (i, 0))
    return pl.pallas_call(
        _merge_kernel,
        grid=(n // tm,),
        in_specs=[row, tok, tok, tok, full((ng, chunk, chunk)), full((ng, chunk, 1)), full((1, width)),
                  full((width, d)), full((width, d))],
        out_specs=row,
        out_shape=jax.ShapeDtypeStruct((n, d), F32),
        scratch_shapes=[pltpu.VMEM((tm, width), BF16)],
        compiler_params=_params(("parallel",)),
        name="merge",
    )(x2d, fox_n, us, vs, w_s, b_s_col, g_sgu_row, wo1, wo2)


def _top_rows(s, k, payload=None):
    nrows = s.shape[0]
    row = lax.broadcasted_iota(jnp.int32, s.shape, 0)
    vals, picks = [], []
    for _ in range(k):
        m = jnp.max(s, axis=0, keepdims=True)
        i = jnp.min(jnp.where(s == m, row, nrows), axis=0, keepdims=True)
        hit = row == i
        vals.append(m)
        picks.append(i if payload is None else jnp.max(jnp.where(hit, payload, -1), axis=0, keepdims=True))
        s = jnp.where(hit, -jnp.inf, s)
    return jnp.concatenate(vals, axis=0), jnp.concatenate(picks, axis=0)


def _route_kernel(y_ref, g_ref, wq_ref, sk_ref, h_ref, idx_ref, gate_ref, hb_scr):
    n_keys = sk_ref.shape[2]
    kk = PEER_TOPK

    @pl.when(pl.program_id(1) == 0)
    def _():
        h = _rms(y_ref[...], g_ref[...])
        h_ref[...] = h
        hb_scr[...] = h.astype(BF16)

    q = jnp.dot(hb_scr[...], wq_ref[...], preferred_element_type=F32).astype(BF16)
    dn = (((1,), (1,)), ((), ()))
    sv, si = [], []
    for c in range(2):
        st = lax.dot_general(sk_ref[0, c], q[:, c * LANES:(c + 1) * LANES], dn, preferred_element_type=F32)
        v, i = _top_rows(st, kk)
        sv.append(v)
        si.append(i)
    cand = jnp.concatenate([sv[0][a:a + 1] + sv[1] for a in range(kk)], axis=0)
    cidx = jnp.concatenate([si[0][a:a + 1] * n_keys + si[1] for a in range(kk)], axis=0)
    top_s, top_i = _top_rows(cand, kk, payload=cidx)
    e = jnp.exp(top_s - top_s[0:1])
    gate_ref[...] = e / jnp.sum(e, axis=0, keepdims=True)
    idx_ref[...] = top_i


def _route(y2d, g_row, w_q, sub_keys, tm):
    n, d = y2d.shape
    n_heads, _, n_keys, half = sub_keys.shape
    assert n_keys == LANES and half == LANES
    sel = n_heads * PEER_TOPK
    return pl.pallas_call(
        _route_kernel,
        grid=(n // tm, n_heads),
        in_specs=[pl.BlockSpec((tm, d), lambda i, hd: (i, 0)),
                  pl.BlockSpec((1, d), lambda i, hd: (0, 0)),
                  pl.BlockSpec((d, 2 * half), lambda i, hd: (0, hd)),
                  pl.BlockSpec((1, 2, n_keys, half), lambda i, hd: (hd, 0, 0, 0))],
        out_specs=[pl.BlockSpec((tm, d), lambda i, hd: (i, 0)),
                   pl.BlockSpec((PEER_TOPK, tm), lambda i, hd: (hd, i)),
                   pl.BlockSpec((PEER_TOPK, tm), lambda i, hd: (hd, i))],
        out_shape=[jax.ShapeDtypeStruct((n, d), F32),
                   jax.ShapeDtypeStruct((sel, n), jnp.int32),
                   jax.ShapeDtypeStruct((sel, n), F32)],
        scratch_shapes=[pltpu.VMEM((tm, d), BF16)],
        compiler_params=_params(("parallel", "arbitrary")),
        name="route",
    )(y2d, g_row, w_q, sub_keys)


def _experts_kernel(idx_ref, h_ref, gate_ref, y_ref, uv_ref, o_ref, buf, sem):
    tb, d = h_ref.shape
    sel = gate_ref.shape[0]

    def row_copy(t, j, slot):
        e = idx_ref[t * sel + j]
        return pltpu.make_async_copy(uv_ref.at[pl.ds(e, 1), :], buf.at[slot, pl.ds(j, 1), :], sem.at[slot])

    def gather(t, slot):
        for j in range(sel):
            row_copy(t, j, slot).start()

    def wait(slot):
        pltpu.make_async_copy(buf.at[slot], buf.at[slot], sem.at[slot]).wait()

    gather(0, 0)
    o_ref[...] = y_ref[...]
    lane = lax.broadcasted_iota(jnp.int32, gate_ref.shape, 1)
    sub = lax.broadcasted_iota(jnp.int32, (SUBLANES, LANES), 0)

    def body(t, carry):
        slot = lax.rem(t, 2)

        @pl.when(t + 1 < tb)
        def _():
            gather(t + 1, 1 - slot)

        wait(slot)
        base = pl.multiple_of((t // SUBLANES) * SUBLANES, SUBLANES)
        mine = sub == (t - base)
        acc = jnp.zeros((sel, LANES), F32)
        for c in range(d // LANES):
            sl = slice(c * LANES, (c + 1) * LANES)
            x = jnp.sum(jnp.where(mine, h_ref[pl.ds(base, SUBLANES), sl], 0.0), axis=0, keepdims=True)
            acc = acc + buf[slot, :, sl] * x
        dots = jnp.sum(acc, axis=-1, keepdims=True)
        gate = jnp.sum(jnp.where(lane == t, gate_ref[...], 0.0), axis=-1, keepdims=True)
        act = jnp.broadcast_to(jax.nn.gelu(dots) * gate, (sel, LANES))
        for c in range(d // LANES):
            sl = slice(c * LANES, (c + 1) * LANES)
            vsl = slice(d + c * LANES, d + (c + 1) * LANES)
            y = jnp.sum(buf[slot, :, vsl] * act, axis=0, keepdims=True)
            o_ref[pl.ds(base, SUBLANES), sl] = o_ref[pl.ds(base, SUBLANES), sl] + jnp.where(mine, y, 0.0)
        return carry

    lax.fori_loop(0, tb, body, 0)


def _experts(idx_flat, h2d, gate_t, y2d, uv, tb):
    n, d = h2d.shape
    sel = gate_t.shape[0]
    return pl.pallas_call(
        _experts_kernel,
        grid=(n // tb,),
        in_specs=[pl.BlockSpec((tb * sel,), lambda i: (i,), memory_space=pltpu.SMEM),
                  pl.BlockSpec((tb, d), lambda i: (i, 0)),
                  pl.BlockSpec((sel, tb), lambda i: (0, i)),
                  pl.BlockSpec((tb, d), lambda i: (i, 0)),
                  pl.BlockSpec(memory_space=pl.ANY)],
        out_specs=pl.BlockSpec((tb, d), lambda i: (i, 0)),
        out_shape=jax.ShapeDtypeStruct((n, d), F32),
        scratch_shapes=[pltpu.VMEM((2, sel, 2 * d), F32), pltpu.SemaphoreType.DMA((2,))],
        compiler_params=_params(("arbitrary",)),
        name="experts",
    )(idx_flat, h2d, gate_t, y2d, uv)


def _pick_tile(n, want):
    t = min(n, want)
    assert n % t == 0, (n, t)
    return t


def _layer_front(x2d, p):
    n, _ = x2d.shape
    tm = _pick_tile(n, 512)
    return _inproj(x2d, p["norm_mix_g"], p["w_main"], p["w_f_t"], p["b_f"], p["q_g"], p["k_g"], p["vs_g"], tm)


def _layer_back(x2d, fox_n, us, vs, p, chunk):
    n, _ = x2d.shape
    w_s = p["w_s"][:, :chunk, :chunk]
    b_s = p["b_s"][:, :chunk, None]
    y1 = _merge(x2d, fox_n, us, vs, w_s, b_s, p["g_sgu"], p["wo1"], p["wo2"], _pick_tile(n, 256))
    h2, idx_t, gate_t = _route(y1, p["norm_ffn_g"], p["w_q"], p["sub_keys"], _pick_tile(n, 256))
    idx_flat = idx_t.T.reshape(-1)
    return _experts(idx_flat, h2, gate_t, y1, p["uv"], _pick_tile(n, 128))


def kernel(x_prompt, x_sample, cache_fox_k, cache_fox_v, cache_fox_logf, norm_mix_g, w_in, b_f, q_norm_g, k_norm_g, sgu_v_norm_g, w_s, b_s, fox_out_norm_g, sgu_out_norm_g, w_out, norm_ffn_g, peer_w_q, peer_sub_keys, peer_u, peer_v):
    depth = w_in.shape[0]
    batch, seq, d = x_prompt.shape
    dec_batch, dec_seq, _ = x_sample.shape
    past = cache_fox_k.shape[2]
    nh, dh = cache_fox_k.shape[3], cache_fox_k.shape[4]
    ng, dg = sgu_v_norm_g.shape[1], sgu_v_norm_g.shape[2]
    assert dh == LANES and dg == LANES and nh == ng
    width = nh * dh
    assert w_in.shape[2] == 3 * width + nh + 2 * width

    y_p = x_prompt.reshape(batch * seq, d)
    y_s = x_sample.reshape(dec_batch * dec_seq, d)
    outs = [[] for _ in range(7)]
    for l in range(depth):
        wl = w_in[l]
        p = {
            "norm_mix_g": norm_mix_g[l][None, :],
            "w_main": jnp.concatenate([wl[:, :3 * width], wl[:, 3 * width + nh:]], axis=1).astype(BF16),
            "w_f_t": wl[:, 3 * width:3 * width + nh].T.astype(BF16),
            "b_f": b_f[l][:, None],
            "q_g": jnp.tile(q_norm_g[l], nh)[None, :],
            "k_g": jnp.tile(k_norm_g[l], nh)[None, :],
            "vs_g": sgu_v_norm_g[l].reshape(1, width),
            "w_s": w_s[l],
            "b_s": b_s[l],
            "g_sgu": sgu_out_norm_g[l].reshape(1, width),
            "wo1": w_out[l][:width].astype(BF16),
            "wo2": w_out[l][width:].astype(BF16),
            "norm_ffn_g": norm_ffn_g[l][None, :],
            "w_q": peer_w_q[l].astype(BF16),
            "sub_keys": peer_sub_keys[l].astype(BF16),
            "uv": jnp.concatenate([peer_u[l], peer_v[l]], axis=1),
        }
        g_fox = fox_out_norm_g[l]

        q, k, v, us, vs, lf_t = _layer_front(y_p, p)
        lf_rows = lf_t.reshape(nh, batch, seq).transpose(1, 0, 2).reshape(batch * nh, seq)
        cum = _cumsum_rows(lf_rows)
        cq_col = cum.reshape(batch, nh, seq).transpose(0, 2, 1).reshape(batch * seq, nh)
        fox_n = _attn_prompt(q, k, v, cq_col, cum[:, None, :], g_fox[:, None, :], batch, seq, _pick_tile(seq, 512))
        y_p = _layer_back(y_p, fox_n, us, vs, p, SGU_CHUNK)
        outs[0].append(k.reshape(batch, seq, nh, dh))
        outs[1].append(v.reshape(batch, seq, nh, dh))
        outs[2].append(lf_t.T.reshape(batch, seq, nh))

        q, k, v, us, vs, lf_t = _layer_front(y_s, p)
        lf_new = lf_t.reshape(nh, dec_batch, dec_seq).transpose(1, 0, 2)
        lf_all = jnp.concatenate([cache_fox_logf[l].astype(F32).transpose(0, 2, 1), lf_new], axis=2)
        total = past + dec_seq
        padded = -(-total // LANES) * LANES
        lf_all = jnp.pad(lf_all, ((0, 0), (0, 0), (0, padded - total))).reshape(dec_batch * nh, padded)
        cum = _cumsum_rows(lf_all).reshape(dec_batch, nh, padded)
        ck_cache = cum[:, :, :past]
        ck_new = cum[:, :, past:total]
        cq_col = ck_new.transpose(0, 2, 1).reshape(dec_batch * dec_seq, nh)
        fox_n = _attn_sample(q, k, v, cache_fox_k[l].reshape(dec_batch, past, width),
                             cache_fox_v[l].reshape(dec_batch, past, width),
                             cq_col, ck_cache, ck_new, g_fox.reshape(1, width), dec_seq)
        y_s = _layer_back(y_s, fox_n, us, vs, p, dec_seq)
        outs[3].append(k.reshape(dec_batch, dec_seq, nh, dh))
        outs[4].append(v.reshape(dec_batch, dec_seq, nh, dh))
        outs[5].append(lf_t.T.reshape(dec_batch, dec_seq, nh))
        outs[6].append(vs.reshape(dec_batch, dec_seq, ng, dg))

    return (y_p.reshape(batch, seq, d), y_s.reshape(dec_batch, dec_seq, d)) + tuple(jnp.stack(o) for o in outs)
```

```python
import functools

import jax
import jax.numpy as jnp
from jax import lax
from jax.experimental import pallas as pl
from jax.experimental.pallas import tpu as pltpu

EPS = 1e-6
NEG_INF = -1e30
LANES = 128
SUBLANES = 8
PEER_TOPK = 16
SGU_CHUNK = 128
VMEM_LIMIT = 52 * 1024 * 1024

F32 = jnp.float32
BF16 = jnp.bfloat16


def _params(sem):
    return pltpu.CompilerParams(dimension_semantics=sem, vmem_limit_bytes=VMEM_LIMIT)


def _rms(x, g):
    return x * lax.rsqrt(jnp.mean(x * x, axis=-1, keepdims=True) + EPS) * g


def _log_sigmoid(x):
    return jnp.minimum(x, 0.0) - jnp.log1p(jnp.exp(-jnp.abs(x)))


def _inproj_kernel(x_ref, g_ref, w_ref, wf_ref, bf_ref, qg_ref, kg_ref, vg_ref,
                   q_ref, k_ref, v_ref, us_ref, vs_ref, lf_ref, h_scr):
    s = pl.program_id(1)
    width = w_ref.shape[1]

    @pl.when(s == 0)
    def _():
        hb = _rms(x_ref[...], g_ref[...]).astype(BF16)
        h_scr[...] = hb
        fl = lax.dot_general(wf_ref[...], hb, (((1,), (1,)), ((), ())), preferred_element_type=F32)
        lf_ref[...] = _log_sigmoid(fl + bf_ref[...])

    z = jnp.dot(h_scr[...], w_ref[...], preferred_element_type=F32)

    def headwise(o_ref, fn):
        for h in range(width // LANES):
            sl = slice(h * LANES, (h + 1) * LANES)
            o_ref[:, sl] = fn(z[:, sl], sl)

    @pl.when(s == 0)
    def _():
        headwise(q_ref, lambda zh, sl: _rms(zh, qg_ref[:, sl]))

    @pl.when(s == 1)
    def _():
        headwise(k_ref, lambda zh, sl: _rms(zh, kg_ref[:, sl]))

    @pl.when(s == 2)
    def _():
        v_ref[...] = z

    @pl.when(s == 3)
    def _():
        us_ref[...] = jax.nn.gelu(z)

    @pl.when(s == 4)
    def _():
        headwise(vs_ref, lambda zh, sl: _rms(jax.nn.gelu(zh), vg_ref[:, sl]))


def _inproj(x2d, norm_g, w_main, w_f_t, b_f, q_g, k_g, vs_g, tm):
    n, d = x2d.shape
    width = w_main.shape[1] // 5
    nh = w_f_t.shape[0]
    full = lambda shape: pl.BlockSpec(shape, lambda i, s: (0,) * len(shape))
    tok = pl.BlockSpec((tm, width), lambda i, s: (i, 0))
    out_sds = jax.ShapeDtypeStruct((n, width), F32)
    return pl.pallas_call(
        _inproj_kernel,
        grid=(n // tm, 5),
        in_specs=[pl.BlockSpec((tm, d), lambda i, s: (i, 0)), full((1, d)),
                  pl.BlockSpec((d, width), lambda i, s: (0, s)), full((nh, d)), full((nh, 1)),
                  full((1, width)), full((1, width)), full((1, width))],
        out_specs=[tok, tok, tok, tok, tok, pl.BlockSpec((nh, tm), lambda i, s: (0, i))],
        out_shape=[out_sds] * 5 + [jax.ShapeDtypeStruct((nh, n), F32)],
        scratch_shapes=[pltpu.VMEM((tm, d), BF16)],
        compiler_params=_params(("parallel", "arbitrary")),
        name="inproj",
    )(x2d, norm_g, w_main, w_f_t, b_f, q_g, k_g, vs_g)


def _cumsum_kernel(x_ref, o_ref):
    rows, length = x_ref.shape
    r = lax.broadcasted_iota(jnp.int32, (LANES, LANES), 0)
    c = lax.broadcasted_iota(jnp.int32, (LANES, LANES), 1)
    tri = jnp.where(r <= c, 1.0, 0.0).astype(BF16)
    carry = jnp.zeros((rows, 1), F32)
    for j in range(length // LANES):
        sl = slice(j * LANES, (j + 1) * LANES)
        x = x_ref[:, sl]
        hi = x.astype(BF16)
        r1 = x - hi.astype(F32)
        mid = r1.astype(BF16)
        lo = (r1 - mid.astype(F32)).astype(BF16)
        cs = (jnp.dot(hi, tri, preferred_element_type=F32)
              + jnp.dot(mid, tri, preferred_element_type=F32)
              + jnp.dot(lo, tri, preferred_element_type=F32))
        out = cs + carry
        o_ref[:, sl] = out
        carry = out[:, LANES - 1:LANES]


def _cumsum_rows(x):
    rows, length = x.shape
    rb = _pick_tile(rows, 8)
    return pl.pallas_call(
        _cumsum_kernel,
        grid=(rows // rb,),
        in_specs=[pl.BlockSpec((rb, length), lambda i: (i, 0))],
        out_specs=pl.BlockSpec((rb, length), lambda i: (i, 0)),
        out_shape=jax.ShapeDtypeStruct((rows, length), F32),
        compiler_params=_params(("parallel",)),
        name="cumsum",
    )(x)


def _attn_prompt_kernel(q_ref, k_ref, v_ref, cq_ref, ck_ref, g_ref, o_ref, m_scr, l_scr, acc_scr, *, scale):
    h = pl.program_id(1)
    qi = pl.program_id(2)
    ki = pl.program_id(3)
    tq = q_ref.shape[0]
    tk = k_ref.shape[0]

    @pl.when(ki == 0)
    def _():
        m_scr[...] = jnp.full(m_scr.shape, NEG_INF, F32)
        l_scr[...] = jnp.zeros(l_scr.shape, F32)
        acc_scr[...] = jnp.zeros(acc_scr.shape, F32)

    def step(masked):
        q = q_ref[...].astype(BF16)
        k = k_ref[...].astype(BF16)
        s = lax.dot_general(q, k, (((1,), (1,)), ((), ())), preferred_element_type=F32) * scale
        lane = lax.broadcasted_iota(jnp.int32, cq_ref.shape, 1)
        cq = jnp.sum(jnp.where(lane == h, cq_ref[...], 0.0), axis=-1, keepdims=True)
        s = s + (cq - ck_ref[0])
        if masked:
            qpos = lax.broadcasted_iota(jnp.int32, (tq, tk), 0)
            kpos = lax.broadcasted_iota(jnp.int32, (tq, tk), 1)
            s = jnp.where(kpos <= qpos, s, NEG_INF)
        m_old = m_scr[...]
        m_new = jnp.maximum(m_old, jnp.max(s, axis=-1, keepdims=True))
        p = jnp.exp(s - m_new)
        alpha = jnp.exp(m_old - m_new)
        l_scr[...] = alpha * l_scr[...] + jnp.sum(p, axis=-1, keepdims=True)
        acc_scr[...] = alpha * acc_scr[...] + jnp.dot(p.astype(BF16), v_ref[...].astype(BF16),
                                                      preferred_element_type=F32)
        m_scr[...] = m_new

    @pl.when(ki < qi)
    def _():
        step(False)

    @pl.when(ki == qi)
    def _():
        step(True)
        o = acc_scr[...] / l_scr[...]
        o_ref[...] = _rms(o, g_ref[0])


def _attn_prompt(q, k, v, cq_col, ck_row, g_fox, batch, seq, tq):
    n, width = q.shape
    nh = width // LANES
    nt = seq // tq
    scale = float(LANES) ** -0.5

    def q_map(b, h, qi, ki):
        return (b * nt + qi, h)

    def kv_map(b, h, qi, ki):
        return (b * nt + jnp.minimum(ki, qi), h)

    return pl.pallas_call(
        functools.partial(_attn_prompt_kernel, scale=scale),
        grid=(batch, nh, nt, nt),
        in_specs=[pl.BlockSpec((tq, LANES), q_map),
                  pl.BlockSpec((tq, LANES), kv_map),
                  pl.BlockSpec((tq, LANES), kv_map),
                  pl.BlockSpec((tq, nh), lambda b, h, qi, ki: (b * nt + qi, 0)),
                  pl.BlockSpec((1, 1, tq), lambda b, h, qi, ki: (b * nh + h, 0, jnp.minimum(ki, qi))),
                  pl.BlockSpec((1, 1, LANES), lambda b, h, qi, ki: (h, 0, 0))],
        out_specs=pl.BlockSpec((tq, LANES), q_map),
        out_shape=jax.ShapeDtypeStruct((n, width), F32),
        scratch_shapes=[pltpu.VMEM((tq, 1), F32), pltpu.VMEM((tq, 1), F32), pltpu.VMEM((tq, LANES), F32)],
        compiler_params=_params(("parallel", "parallel", "parallel", "arbitrary")),
        name="attn_prompt",
    )(q, k, v, cq_col, ck_row, g_fox)


def _attn_sample_kernel(q_ref, kn_ref, vn_ref, kc_ref, vc_ref, cq_ref, ckc_ref, ckn_ref, g_ref, o_ref, *, scale):
    ts = q_ref.shape[0]
    nh = q_ref.shape[1] // LANES
    dn = (((1,), (1,)), ((), ()))
    qpos = lax.broadcasted_iota(jnp.int32, (ts, ts), 0)
    kpos = lax.broadcasted_iota(jnp.int32, (ts, ts), 1)
    for h in range(nh):
        sl = slice(h * LANES, (h + 1) * LANES)
        q = q_ref[:, sl].astype(BF16)
        cq = cq_ref[:, h:h + 1]
        s1 = lax.dot_general(q, kc_ref[0, :, sl].astype(BF16), dn, preferred_element_type=F32) * scale
        s1 = s1 + (cq - ckc_ref[0, h:h + 1, :])
        s2 = lax.dot_general(q, kn_ref[:, sl].astype(BF16), dn, preferred_element_type=F32) * scale
        s2 = jnp.where(kpos <= qpos, s2 + (cq - ckn_ref[0, h:h + 1, :]), NEG_INF)
        m = jnp.maximum(jnp.max(s1, axis=-1, keepdims=True), jnp.max(s2, axis=-1, keepdims=True))
        p1 = jnp.exp(s1 - m)
        p2 = jnp.exp(s2 - m)
        l = jnp.sum(p1, axis=-1, keepdims=True) + jnp.sum(p2, axis=-1, keepdims=True)
        o = (jnp.dot(p1.astype(BF16), vc_ref[0, :, sl].astype(BF16), preferred_element_type=F32)
             + jnp.dot(p2.astype(BF16), vn_ref[:, sl].astype(BF16), preferred_element_type=F32)) / l
        o_ref[:, sl] = _rms(o, g_ref[:, sl])


def _attn_sample(q, k_new, v_new, k_cache, v_cache, cq_col, ck_cache, ck_new, g_fox_row, ts):
    n, width = q.shape
    nb, past, _ = k_cache.shape
    nh = width // LANES
    scale = float(LANES) ** -0.5
    tok = pl.BlockSpec((ts, width), lambda b: (b, 0))
    cache = pl.BlockSpec((1, past, width), lambda b: (b, 0, 0))
    return pl.pallas_call(
        functools.partial(_attn_sample_kernel, scale=scale),
        grid=(nb,),
        in_specs=[tok, tok, tok, cache, cache,
                  pl.BlockSpec((ts, nh), lambda b: (b, 0)),
                  pl.BlockSpec((1, nh, past), lambda b: (b, 0, 0)),
                  pl.BlockSpec((1, nh, ts), lambda b: (b, 0, 0)),
                  pl.BlockSpec((1, width), lambda b: (0, 0))],
        out_specs=tok,
        out_shape=jax.ShapeDtypeStruct((n, width), F32),
        compiler_params=_params(("parallel",)),
        name="attn_sample",
    )(q, k_new, v_new, k_cache, v_cache, cq_col, ck_cache, ck_new, g_fox_row)


def _merge_kernel(x_ref, fox_ref, us_ref, vs_ref, ws_ref, bs_ref, gs_ref, wo1_ref, wo2_ref, o_ref, cat_scr):
    tm = x_ref.shape[0]
    ng, chunk, _ = ws_ref.shape
    r = lax.broadcasted_iota(jnp.int32, (chunk, chunk), 0)
    c = lax.broadcasted_iota(jnp.int32, (chunk, chunk), 1)
    for g in range(ng):
        sl = slice(g * LANES, (g + 1) * LANES)
        wg = jnp.where(c <= r, ws_ref[g], 0.0).astype(BF16)
        bias = bs_ref[g]
        for j in range(tm // chunk):
            rows = slice(j * chunk, (j + 1) * chunk)
            mixed = jnp.dot(wg, vs_ref[rows, sl].astype(BF16), preferred_element_type=F32) + bias
            cat_scr[rows, sl] = _rms(us_ref[rows, sl] * mixed, gs_ref[:, sl]).astype(BF16)
    o_ref[...] = (x_ref[...]
                  + jnp.dot(fox_ref[...].astype(BF16), wo1_ref[...], preferred_element_type=F32)
                  + jnp.dot(cat_scr[...], wo2_ref[...], preferred_element_type=F32))


def _merge(x2d, fox_n, us, vs, w_s, b_s_col, g_sgu_row, wo1, wo2, tm):
    n, d = x2d.shape
    width = fox_n.shape[1]
    ng, chunk, _ = w_s.shape
    full = lambda shape: pl.BlockSpec(shape, lambda i: (0,) * len(shape))
    tok = pl.BlockSpec((tm, width), lambda i: (i, 0))
    row = pl.BlockSpec((tm, d), lambda i: (i, 0))
    return pl.pallas_call(
        _merge_kernel,
        grid=(n // tm,),
        in_specs=[row, tok, tok, tok, full((ng, chunk, chunk)), full((ng, chunk, 1)), full((1, width)),
                  full((width, d)), full((width, d))],
        out_specs=row,
        out_shape=jax.ShapeDtypeStruct((n, d), F32),
        scratch_shapes=[pltpu.VMEM((tm, width), BF16)],
        compiler_params=_params(("parallel",)),
        name="merge",
    )(x2d, fox_n, us, vs, w_s, b_s_col, g_sgu_row, wo1, wo2)


def _top_rows(s, k, payload=None):
    nrows = s.shape[0]
    row = lax.broadcasted_iota(jnp.int32, s.shape, 0)
    vals, picks = [], []
    for _ in range(k):
        m = jnp.max(s, axis=0, keepdims=True)
        i = jnp.min(jnp.where(s == m, row, nrows), axis=0, keepdims=True)
        hit = row == i
        vals.append(m)
        picks.append(i if payload is None else jnp.max(jnp.where(hit, payload, -1), axis=0, keepdims=True))
        s = jnp.where(hit, -jnp.inf, s)
    return jnp.concatenate(vals, axis=0), jnp.concatenate(picks, axis=0)


def _route_kernel(y_ref, g_ref, wq_ref, sk_ref, h_ref, idx_ref, gate_ref, hb_scr):
    n_keys = sk_ref.shape[2]
    kk = PEER_TOPK

    @pl.when(pl.program_id(1) == 0)
    def _():
        h = _rms(y_ref[...], g_ref[...])
        h_ref[...] = h
        hb_scr[...] = h.astype(BF16)

    q = jnp.dot(hb_scr[...], wq_ref[...], preferred_element_type=F32).astype(BF16)
    dn = (((1,), (1,)), ((), ()))
    sv, si = [], []
    for c in range(2):
        st = lax.dot_general(sk_ref[0, c], q[:, c * LANES:(c + 1) * LANES], dn, preferred_element_type=F32)
        v, i = _top_rows(st, kk)
        sv.append(v)
        si.append(i)
    cand = jnp.concatenate([sv[0][a:a + 1] + sv[1] for a in range(kk)], axis=0)
    cidx = jnp.concatenate([si[0][a:a + 1] * n_keys + si[1] for a in range(kk)], axis=0)
    top_s, top_i = _top_rows(cand, kk, payload=cidx)
    e = jnp.exp(top_s - top_s[0:1])
    gate_ref[...] = e / jnp.sum(e, axis=0, keepdims=True)
    idx_ref[...] = top_i


def _route(y2d, g_row, w_q, sub_keys, tm):
    n, d = y2d.shape
    n_heads, _, n_keys, half = sub_keys.shape
    assert n_keys == LANES and half == LANES
    sel = n_heads * PEER_TOPK
    return pl.pallas_call(
        _route_kernel,
        grid=(n // tm, n_heads),
        in_specs=[pl.BlockSpec((tm, d), lambda i, hd: (i, 0)),
                  pl.BlockSpec((1, d), lambda i, hd: (0, 0)),
                  pl.BlockSpec((d, 2 * half), lambda i, hd: (0, hd)),
                  pl.BlockSpec((1, 2, n_keys, half), lambda i, hd: (hd, 0, 0, 0))],
        out_specs=[pl.BlockSpec((tm, d), lambda i, hd: (i, 0)),
                   pl.BlockSpec((PEER_TOPK, tm), lambda i, hd: (hd, i)),
                   pl.BlockSpec((PEER_TOPK, tm), lambda i, hd: (hd, i))],
        out_shape=[jax.ShapeDtypeStruct((n, d), F32),
                   jax.ShapeDtypeStruct((sel, n), jnp.int32),
                   jax.ShapeDtypeStruct((sel, n), F32)],
        scratch_shapes=[pltpu.VMEM((tm, d), BF16)],
        compiler_params=_params(("parallel", "arbitrary")),
        name="route",
    )(y2d, g_row, w_q, sub_keys)


def _experts_kernel(idx_ref, h_ref, gate_ref, y_ref, uv_ref, o_ref, buf0, buf1, sem):
    tb, d = h_ref.shape
    sel = gate_ref.shape[0]
    bufs = (buf0, buf1)

    def gather(t, slot):
        for j in range(sel):
            e = idx_ref[t * sel + j]
            pltpu.make_async_copy(uv_ref.at[e], bufs[slot].at[pl.ds(j, 1), :], sem.at[slot]).start()

    def wait(slot):
        pltpu.make_async_copy(bufs[slot], bufs[slot], sem.at[slot]).wait()

    gather(0, 0)
    lane = lax.broadcasted_iota(jnp.int32, gate_ref.shape, 1)
    half = d // 2

    def unpack(w):
        return (lax.bitcast_convert_type(w << 16, F32), lax.bitcast_convert_type(w & jnp.int32(-65536), F32))

    def group(g, carry):
        base = pl.multiple_of(g * SUBLANES, SUBLANES)
        x8 = h_ref[pl.ds(base, SUBLANES), :]
        rows = []
        for i in range(SUBLANES):
            t = base + i
            slot = i % 2
            buf = bufs[slot]
            gather(jnp.minimum(t + 1, tb - 1), 1 - slot)
            wait(slot)
            acc = jnp.zeros((sel, LANES), F32)
            for c in range(half // LANES):
                lo, hi = unpack(buf[:, c * LANES:(c + 1) * LANES])
                acc = (acc + lo * x8[i:i + 1, c * LANES:(c + 1) * LANES]
                       + hi * x8[i:i + 1, half + c * LANES:half + (c + 1) * LANES])
            dots = jnp.sum(acc, axis=-1, keepdims=True)
            gate = jnp.sum(jnp.where(lane == t, gate_ref[...], 0.0), axis=-1, keepdims=True)
            act = jnp.broadcast_to(jax.nn.gelu(dots) * gate, (sel, LANES))
            y_lo, y_hi = [], []
            for c in range(half // LANES):
                lo, hi = unpack(buf[:, half + c * LANES:half + (c + 1) * LANES])
                y_lo.append(jnp.sum(lo * act, axis=0, keepdims=True))
                y_hi.append(jnp.sum(hi * act, axis=0, keepdims=True))
            rows.append(jnp.concatenate(y_lo + y_hi, axis=1))
        o_ref[pl.ds(base, SUBLANES), :] = y_ref[pl.ds(base, SUBLANES), :] + jnp.concatenate(rows, axis=0)
        return carry

    lax.fori_loop(0, tb // SUBLANES, group, 0)
    wait(0)


def _experts(idx_flat, h2d, gate_t, y2d, uv, tb):
    n, d = h2d.shape
    sel = gate_t.shape[0]
    assert tb % (2 * SUBLANES) == 0
    return pl.pallas_call(
        _experts_kernel,
        grid=(n // tb,),
        in_specs=[pl.BlockSpec((tb * sel,), lambda i: (i,), memory_space=pltpu.SMEM),
                  pl.BlockSpec((tb, d), lambda i: (i, 0)),
                  pl.BlockSpec((sel, tb), lambda i: (0, i)),
                  pl.BlockSpec((tb, d), lambda i: (i, 0)),
                  pl.BlockSpec(memory_space=pl.ANY)],
        out_specs=pl.BlockSpec((tb, d), lambda i: (i, 0)),
        out_shape=jax.ShapeDtypeStruct((n, d), F32),
        scratch_shapes=[pltpu.VMEM((sel, d), jnp.int32), pltpu.VMEM((sel, d), jnp.int32),
                        pltpu.SemaphoreType.DMA((2,))],
        compiler_params=_params(("arbitrary",)),
        name="experts",
    )(idx_flat, h2d, gate_t, y2d, uv)


def _pack_bf16_halves(x):
    half = x.shape[1] // 2
    bits = lax.bitcast_convert_type(x.astype(BF16), jnp.uint16).astype(jnp.uint32)
    return lax.bitcast_convert_type(bits[:, :half] | (bits[:, half:] << 16), jnp.int32)


def _pick_tile(n, want):
    t = min(n, want)
    assert n % t == 0, (n, t)
    return t


def _layer_front(x2d, p):
    n, _ = x2d.shape
    tm = _pick_tile(n, 512)
    return _inproj(x2d, p["norm_mix_g"], p["w_main"], p["w_f_t"], p["b_f"], p["q_g"], p["k_g"], p["vs_g"], tm)


def _layer_back(x2d, fox_n, us, vs, p, chunk):
    n, _ = x2d.shape
    w_s = p["w_s"][:, :chunk, :chunk]
    b_s = p["b_s"][:, :chunk, None]
    y1 = _merge(x2d, fox_n, us, vs, w_s, b_s, p["g_sgu"], p["wo1"], p["wo2"], _pick_tile(n, 256))
    h2, idx_t, gate_t = _route(y1, p["norm_ffn_g"], p["w_q"], p["sub_keys"], _pick_tile(n, 256))
    idx_flat = idx_t.T.reshape(-1)
    return _experts(idx_flat, h2, gate_t, y1, p["uv"], _pick_tile(n, 128))


def kernel(x_prompt, x_sample, cache_fox_k, cache_fox_v, cache_fox_logf, norm_mix_g, w_in, b_f, q_norm_g, k_norm_g, sgu_v_norm_g, w_s, b_s, fox_out_norm_g, sgu_out_norm_g, w_out, norm_ffn_g, peer_w_q, peer_sub_keys, peer_u, peer_v):
    depth = w_in.shape[0]
    batch, seq, d = x_prompt.shape
    dec_batch, dec_seq, _ = x_sample.shape
    past = cache_fox_k.shape[2]
    nh, dh = cache_fox_k.shape[3], cache_fox_k.shape[4]
    ng, dg = sgu_v_norm_g.shape[1], sgu_v_norm_g.shape[2]
    assert dh == LANES and dg == LANES and nh == ng
    width = nh * dh
    assert w_in.shape[2] == 3 * width + nh + 2 * width

    y_p = x_prompt.reshape(batch * seq, d)
    y_s = x_sample.reshape(dec_batch * dec_seq, d)
    outs = [[] for _ in range(7)]
    for l in range(depth):
        wl = w_in[l]
        p = {
            "norm_mix_g": norm_mix_g[l][None, :],
            "w_main": jnp.concatenate([wl[:, :3 * width], wl[:, 3 * width + nh:]], axis=1).astype(BF16),
            "w_f_t": wl[:, 3 * width:3 * width + nh].T.astype(BF16),
            "b_f": b_f[l][:, None],
            "q_g": jnp.tile(q_norm_g[l], nh)[None, :],
            "k_g": jnp.tile(k_norm_g[l], nh)[None, :],
            "vs_g": sgu_v_norm_g[l].reshape(1, width),
            "w_s": w_s[l],
            "b_s": b_s[l],
            "g_sgu": sgu_out_norm_g[l].reshape(1, width),
            "wo1": w_out[l][:width].astype(BF16),
            "wo2": w_out[l][width:].astype(BF16),
            "norm_ffn_g": norm_ffn_g[l][None, :],
            "w_q": peer_w_q[l].astype(BF16),
            "sub_keys": peer_sub_keys[l].astype(BF16),
            "uv": jnp.concatenate([_pack_bf16_halves(peer_u[l]), _pack_bf16_halves(peer_v[l])], axis=1)[:, None, :],
        }
        g_fox = fox_out_norm_g[l]

        q, k, v, us, vs, lf_t = _layer_front(y_s, p)
        lf_new = lf_t.reshape(nh, dec_batch, dec_seq).transpose(1, 0, 2)
        lf_all = jnp.concatenate([cache_fox_logf[l].astype(F32).transpose(0, 2, 1), lf_new], axis=2)
        total = past + dec_seq
        padded = -(-total // LANES) * LANES
        lf_all = jnp.pad(lf_all, ((0, 0), (0, 0), (0, padded - total))).reshape(dec_batch * nh, padded)
        cum = _cumsum_rows(lf_all).reshape(dec_batch, nh, padded)
        ck_cache = cum[:, :, :past]
        ck_new = cum[:, :, past:total]
        cq_col = ck_new.transpose(0, 2, 1).reshape(dec_batch * dec_seq, nh)
        fox_n = _attn_sample(q, k, v, cache_fox_k[l].reshape(dec_batch, past, width),
                             cache_fox_v[l].reshape(dec_batch, past, width),
                             cq_col, ck_cache, ck_new, g_fox.reshape(1, width), dec_seq)
        y_s = _layer_back(y_s, fox_n, us, vs, p, dec_seq)
        outs[3].append(k.reshape(dec_batch, dec_seq, nh, dh))
        outs[4].append(v.reshape(dec_batch, dec_seq, nh, dh))
        outs[5].append(lf_t.T.reshape(dec_batch, dec_seq, nh))
        outs[6].append(vs.reshape(dec_batch, dec_seq, ng, dg))

        q, k, v, us, vs, lf_t = _layer_front(y_p, p)
        lf_rows = lf_t.reshape(nh, batch, seq).transpose(1, 0, 2).reshape(batch * nh, seq)
        cum = _cumsum_rows(lf_rows)
        cq_col = cum.reshape(batch, nh, seq).transpose(0, 2, 1).reshape(batch * seq, nh)
        fox_n = _attn_prompt(q, k, v, cq_col, cum[:, None, :], g_fox[:, None, :], batch, seq, _pick_tile(seq, 512))
        y_p = _layer_back(y_p, fox_n, us, vs, p, SGU_CHUNK)
        outs[0].append(k.reshape(batch, seq, nh, dh))
        outs[1].append(v.reshape(batch, seq, nh, dh))
        outs[2].append(lf_t.T.reshape(batch, seq, nh))

    return (y_p.reshape(batch, seq, d), y_s.reshape(dec_batch, dec_seq, d)) + tuple(jnp.stack(o) for o in outs)
```

```python
import functools

import jax
import jax.numpy as jnp
from jax import lax
from jax.experimental import pallas as pl
from jax.experimental.pallas import tpu as pltpu

EPS = 1e-6
NEG_INF = -1e30
LANES = 128
SUBLANES = 8
PEER_TOPK = 16
SGU_CHUNK = 128
EXPERT_SLOTS = 4
VMEM_LIMIT = 52 * 1024 * 1024

F32 = jnp.float32
BF16 = jnp.bfloat16


def _params(sem):
    return pltpu.CompilerParams(dimension_semantics=sem, vmem_limit_bytes=VMEM_LIMIT)


def _rms(x, g):
    return x * lax.rsqrt(jnp.mean(x * x, axis=-1, keepdims=True) + EPS) * g


def _log_sigmoid(x):
    return jnp.minimum(x, 0.0) - jnp.log1p(jnp.exp(-jnp.abs(x)))


def _inproj_kernel(x_ref, g_ref, w_ref, wf_ref, bf_ref, qg_ref, kg_ref, vg_ref,
                   q_ref, k_ref, v_ref, us_ref, vs_ref, lf_ref, h_scr):
    s = pl.program_id(1)
    width = w_ref.shape[1]

    @pl.when(s == 0)
    def _():
        hb = _rms(x_ref[...], g_ref[...]).astype(BF16)
        h_scr[...] = hb
        fl = lax.dot_general(wf_ref[...], hb, (((1,), (1,)), ((), ())), preferred_element_type=F32)
        lf_ref[...] = _log_sigmoid(fl + bf_ref[...])

    z = jnp.dot(h_scr[...], w_ref[...], preferred_element_type=F32)

    def headwise(o_ref, fn):
        for h in range(width // LANES):
            sl = slice(h * LANES, (h + 1) * LANES)
            o_ref[:, sl] = fn(z[:, sl], sl)

    @pl.when(s == 0)
    def _():
        headwise(q_ref, lambda zh, sl: _rms(zh, qg_ref[:, sl]))

    @pl.when(s == 1)
    def _():
        headwise(k_ref, lambda zh, sl: _rms(zh, kg_ref[:, sl]))

    @pl.when(s == 2)
    def _():
        v_ref[...] = z

    @pl.when(s == 3)
    def _():
        us_ref[...] = jax.nn.gelu(z)

    @pl.when(s == 4)
    def _():
        headwise(vs_ref, lambda zh, sl: _rms(jax.nn.gelu(zh), vg_ref[:, sl]))


def _inproj(x2d, norm_g, w_main, w_f_t, b_f, q_g, k_g, vs_g, tm):
    n, d = x2d.shape
    width = w_main.shape[1] // 5
    nh = w_f_t.shape[0]
    full = lambda shape: pl.BlockSpec(shape, lambda i, s: (0,) * len(shape))
    tok = pl.BlockSpec((tm, width), lambda i, s: (i, 0))
    out_sds = jax.ShapeDtypeStruct((n, width), F32)
    return pl.pallas_call(
        _inproj_kernel,
        grid=(n // tm, 5),
        in_specs=[pl.BlockSpec((tm, d), lambda i, s: (i, 0)), full((1, d)),
                  pl.BlockSpec((d, width), lambda i, s: (0, s)), full((nh, d)), full((nh, 1)),
                  full((1, width)), full((1, width)), full((1, width))],
        out_specs=[tok, tok, tok, tok, tok, pl.BlockSpec((nh, tm), lambda i, s: (0, i))],
        out_shape=[out_sds] * 5 + [jax.ShapeDtypeStruct((nh, n), F32)],
        scratch_shapes=[pltpu.VMEM((tm, d), BF16)],
        compiler_params=_params(("parallel", "arbitrary")),
        name="inproj",
    )(x2d, norm_g, w_main, w_f_t, b_f, q_g, k_g, vs_g)


def _cumsum_kernel(x_ref, o_ref):
    rows, length = x_ref.shape
    r = lax.broadcasted_iota(jnp.int32, (LANES, LANES), 0)
    c = lax.broadcasted_iota(jnp.int32, (LANES, LANES), 1)
    tri = jnp.where(r <= c, 1.0, 0.0).astype(BF16)
    carry = jnp.zeros((rows, 1), F32)
    for j in range(length // LANES):
        sl = slice(j * LANES, (j + 1) * LANES)
        x = x_ref[:, sl]
        hi = x.astype(BF16)
        r1 = x - hi.astype(F32)
        mid = r1.astype(BF16)
        lo = (r1 - mid.astype(F32)).astype(BF16)
        cs = (jnp.dot(hi, tri, preferred_element_type=F32)
              + jnp.dot(mid, tri, preferred_element_type=F32)
              + jnp.dot(lo, tri, preferred_element_type=F32))
        out = cs + carry
        o_ref[:, sl] = out
        carry = out[:, LANES - 1:LANES]


def _cumsum_rows(x):
    rows, length = x.shape
    rb = _pick_tile(rows, 8)
    return pl.pallas_call(
        _cumsum_kernel,
        grid=(rows // rb,),
        in_specs=[pl.BlockSpec((rb, length), lambda i: (i, 0))],
        out_specs=pl.BlockSpec((rb, length), lambda i: (i, 0)),
        out_shape=jax.ShapeDtypeStruct((rows, length), F32),
        compiler_params=_params(("parallel",)),
        name="cumsum",
    )(x)


def _split3(x):
    hi = x.astype(BF16).astype(F32)
    r = x - hi
    mid = r.astype(BF16).astype(F32)
    return hi, mid, (r - mid).astype(BF16).astype(F32)


def _decay_lanes(c, query_side):
    rows = c.shape[0]
    lane = lax.broadcasted_iota(jnp.int32, (rows, LANES), 1)
    hi, mid, lo = _split3(c if query_side else -c)
    if query_side:
        out = jnp.where(lane == 0, hi, jnp.where(lane == 1, mid, jnp.where(lane == 2, lo,
                                                                          jnp.where(lane < 6, 1.0, 0.0))))
    else:
        out = jnp.where(lane < 3, 1.0, jnp.where(lane == 3, hi, jnp.where(lane == 4, mid,
                                                                         jnp.where(lane == 5, lo, 0.0))))
    return out.astype(BF16)


def _attn_prompt_kernel(q_ref, k_ref, v_ref, cum_ref, g_ref, o_ref, qa_scr, ka_scr, va_scr, m_scr, acc_scr, *, scale):
    h = pl.program_id(1)
    qi = pl.program_id(2)
    tq = q_ref.shape[0]
    seq = k_ref.shape[0]
    nh = cum_ref.shape[1]
    c1 = scale * 1.4426950408889634
    head_lane = lax.broadcasted_iota(jnp.int32, (tq, nh), 1)

    def head_col(x):
        return jnp.sum(jnp.where(head_lane == h, x, 0.0), axis=-1, keepdims=True) * (1.0 / scale)

    @pl.when(qi == 0)
    def _():
        for j in range(seq // tq):
            rows = slice(j * tq, (j + 1) * tq)
            ka_scr[rows, :LANES] = k_ref[rows, :].astype(BF16)
            ka_scr[rows, LANES:] = _decay_lanes(head_col(cum_ref[rows, :]), False)
            va_scr[rows, :LANES] = v_ref[rows, :].astype(BF16)
            va_scr[rows, LANES:] = jnp.ones((tq, LANES), BF16)

    q0 = pl.multiple_of(qi * tq, tq)
    qa_scr[:, :LANES] = q_ref[...].astype(BF16)
    qa_scr[:, LANES:] = _decay_lanes(head_col(cum_ref[pl.ds(q0, tq), :]), True)
    m_scr[...] = jnp.full(m_scr.shape, NEG_INF, F32)
    acc_scr[...] = jnp.zeros(acc_scr.shape, F32)

    def tile(ki, masked):
        k0 = pl.multiple_of(ki * tq, tq)
        s = lax.dot_general(qa_scr[...], ka_scr[pl.ds(k0, tq), :], (((1,), (1,)), ((), ())),
                            preferred_element_type=F32)
        if masked:
            qpos = lax.broadcasted_iota(jnp.int32, (tq, tq), 0)
            kpos = lax.broadcasted_iota(jnp.int32, (tq, tq), 1)
            s = jnp.where(kpos <= qpos, s, NEG_INF)
        m_old = m_scr[...]
        m_new = jnp.maximum(m_old, jnp.max(s, axis=-1, keepdims=True))
        p = jnp.exp2(s * c1 - m_new * c1)
        alpha = jnp.exp2((m_old - m_new) * c1)
        acc_scr[...] = alpha * acc_scr[...] + jnp.dot(p.astype(BF16), va_scr[pl.ds(k0, tq), :],
                                                      preferred_element_type=F32)
        m_scr[...] = m_new

    def body(ki, carry):
        tile(ki, False)
        return carry

    lax.fori_loop(0, qi, body, 0)
    tile(qi, True)
    o_ref[...] = _rms(acc_scr[:, :LANES] / acc_scr[:, LANES:], g_ref[0])


def _attn_prompt(q, k, v, cum_col, g_fox, batch, seq, tq):
    n, width = q.shape
    nh = width // LANES
    nt = seq // tq
    scale = float(LANES) ** -0.5
    return pl.pallas_call(
        functools.partial(_attn_prompt_kernel, scale=scale),
        grid=(batch, nh, nt),
        in_specs=[pl.BlockSpec((tq, LANES), lambda b, h, qi: (b * nt + qi, h)),
                  pl.BlockSpec((seq, LANES), lambda b, h, qi: (b, h)),
                  pl.BlockSpec((seq, LANES), lambda b, h, qi: (b, h)),
                  pl.BlockSpec((seq, nh), lambda b, h, qi: (b, 0)),
                  pl.BlockSpec((1, 1, LANES), lambda b, h, qi: (h, 0, 0))],
        out_specs=pl.BlockSpec((tq, LANES), lambda b, h, qi: (b * nt + qi, h)),
        out_shape=jax.ShapeDtypeStruct((n, width), F32),
        scratch_shapes=[pltpu.VMEM((tq, 2 * LANES), BF16), pltpu.VMEM((seq, 2 * LANES), BF16),
                        pltpu.VMEM((seq, 2 * LANES), BF16), pltpu.VMEM((tq, 1), F32),
                        pltpu.VMEM((tq, 2 * LANES), F32)],
        compiler_params=_params(("parallel", "parallel", "arbitrary")),
        name="attn_prompt",
    )(q, k, v, cum_col, g_fox)


def _attn_sample_kernel(q_ref, kn_ref, vn_ref, kc_ref, vc_ref, cq_ref, ckc_ref, ckn_ref, g_ref, o_ref, *, scale):
    ts = q_ref.shape[0]
    nh = q_ref.shape[1] // LANES
    dn = (((1,), (1,)), ((), ()))
    qpos = lax.broadcasted_iota(jnp.int32, (ts, ts), 0)
    kpos = lax.broadcasted_iota(jnp.int32, (ts, ts), 1)
    for h in range(nh):
        sl = slice(h * LANES, (h + 1) * LANES)
        q = q_ref[:, sl].astype(BF16)
        cq = cq_ref[:, h:h + 1]
        s1 = lax.dot_general(q, kc_ref[0, :, sl].astype(BF16), dn, preferred_element_type=F32) * scale
        s1 = s1 + (cq - ckc_ref[0, h:h + 1, :])
        s2 = lax.dot_general(q, kn_ref[:, sl].astype(BF16), dn, preferred_element_type=F32) * scale
        s2 = jnp.where(kpos <= qpos, s2 + (cq - ckn_ref[0, h:h + 1, :]), NEG_INF)
        m = jnp.maximum(jnp.max(s1, axis=-1, keepdims=True), jnp.max(s2, axis=-1, keepdims=True))
        p1 = jnp.exp(s1 - m)
        p2 = jnp.exp(s2 - m)
        l = jnp.sum(p1, axis=-1, keepdims=True) + jnp.sum(p2, axis=-1, keepdims=True)
        o = (jnp.dot(p1.astype(BF16), vc_ref[0, :, sl].astype(BF16), preferred_element_type=F32)
             + jnp.dot(p2.astype(BF16), vn_ref[:, sl].astype(BF16), preferred_element_type=F32)) / l
        o_ref[:, sl] = _rms(o, g_ref[:, sl])


def _attn_sample(q, k_new, v_new, k_cache, v_cache, cq_col, ck_cache, ck_new, g_fox_row, ts):
    n, width = q.shape
    nb, past, _ = k_cache.shape
    nh = width // LANES
    scale = float(LANES) ** -0.5
    tok = pl.BlockSpec((ts, width), lambda b: (b, 0))
    cache = pl.BlockSpec((1, past, width), lambda b: (b, 0, 0))
    return pl.pallas_call(
        functools.partial(_attn_sample_kernel, scale=scale),
        grid=(nb,),
        in_specs=[tok, tok, tok, cache, cache,
                  pl.BlockSpec((ts, nh), lambda b: (b, 0)),
                  pl.BlockSpec((1, nh, past), lambda b: (b, 0, 0)),
                  pl.BlockSpec((1, nh, ts), lambda b: (b, 0, 0)),
                  pl.BlockSpec((1, width), lambda b: (0, 0))],
        out_specs=tok,
        out_shape=jax.ShapeDtypeStruct((n, width), F32),
        compiler_params=_params(("parallel",)),
        name="attn_sample",
    )(q, k_new, v_new, k_cache, v_cache, cq_col, ck_cache, ck_new, g_fox_row)


def _merge_kernel(x_ref, fox_ref, us_ref, vs_ref, ws_ref, bs_ref, gs_ref, wo1_ref, wo2_ref, o_ref, cat_scr):
    tm = x_ref.shape[0]
    ng, chunk, _ = ws_ref.shape
    r = lax.broadcasted_iota(jnp.int32, (chunk, chunk), 0)
    c = lax.broadcasted_iota(jnp.int32, (chunk, chunk), 1)
    for g in range(ng):
        sl = slice(g * LANES, (g + 1) * LANES)
        wg = jnp.where(c <= r, ws_ref[g], 0.0).astype(BF16)
        bias = bs_ref[g]
        for j in range(tm // chunk):
            rows = slice(j * chunk, (j + 1) * chunk)
            mixed = jnp.dot(wg, vs_ref[rows, sl].astype(BF16), preferred_element_type=F32) + bias
            cat_scr[rows, sl] = _rms(us_ref[rows, sl] * mixed, gs_ref[:, sl]).astype(BF16)
    o_ref[...] = (x_ref[...]
                  + jnp.dot(fox_ref[...].astype(BF16), wo1_ref[...], preferred_element_type=F32)
                  + jnp.dot(cat_scr[...], wo2_ref[...], preferred_element_type=F32))


def _merge(x2d, fox_n, us, vs, w_s, b_s_col, g_sgu_row, wo1, wo2, tm):
    n, d = x2d.shape
    width = fox_n.shape[1]
    ng, chunk, _ = w_s.shape
    full = lambda shape: pl.BlockSpec(shape, lambda i: (0,) * len(shape))
    tok = pl.BlockSpec((tm, width), lambda i: (i, 0))
    row = pl.BlockSpec((tm, d), lambda i: (i, 0))
    return pl.pallas_call(
        _merge_kernel,
        grid=(n // tm,),
        in_specs=[row, tok, tok, tok, full((ng, chunk, chunk)), full((ng, chunk, 1)), full((1, width)),
                  full((width, d)), full((width, d))],
        out_specs=row,
        out_shape=jax.ShapeDtypeStruct((n, d), F32),
        scratch_shapes=[pltpu.VMEM((tm, width), BF16)],
        compiler_params=_params(("parallel",)),
        name="merge",
    )(x2d, fox_n, us, vs, w_s, b_s_col, g_sgu_row, wo1, wo2)


def _top_rows(s, k, payload=None):
    nrows = s.shape[0]
    row = lax.broadcasted_iota(jnp.int32, s.shape, 0)
    vals, picks = [], []
    for _ in range(k):
        m = jnp.max(s, axis=0, keepdims=True)
        i = jnp.min(jnp.where(s == m, row, nrows), axis=0, keepdims=True)
        hit = row == i
        vals.append(m)
        picks.append(i if payload is None else jnp.max(jnp.where(hit, payload, -1), axis=0, keepdims=True))
        s = jnp.where(hit, -jnp.inf, s)
    return jnp.concatenate(vals, axis=0), jnp.concatenate(picks, axis=0)


def _route_kernel(y_ref, g_ref, wq_ref, sk_ref, h_ref, idx_ref, gate_ref, hb_scr):
    n_keys = sk_ref.shape[2]
    kk = PEER_TOPK

    @pl.when(pl.program_id(1) == 0)
    def _():
        h = _rms(y_ref[...], g_ref[...])
        h_ref[...] = h
        hb_scr[...] = h.astype(BF16)

    q = jnp.dot(hb_scr[...], wq_ref[...], preferred_element_type=F32).astype(BF16)
    dn = (((1,), (1,)), ((), ()))
    sv, si = [], []
    for c in range(2):
        st = lax.dot_general(sk_ref[0, c], q[:, c * LANES:(c + 1) * LANES], dn, preferred_element_type=F32)
        v, i = _top_rows(st, kk)
        sv.append(v)
        si.append(i)
    cand = jnp.concatenate([sv[0][a:a + 1] + sv[1] for a in range(kk)], axis=0)
    cidx = jnp.concatenate([si[0][a:a + 1] * n_keys + si[1] for a in range(kk)], axis=0)
    top_s, top_i = _top_rows(cand, kk, payload=cidx)
    e = jnp.exp(top_s - top_s[0:1])
    gate_ref[...] = e / jnp.sum(e, axis=0, keepdims=True)
    idx_ref[...] = top_i


def _route(y2d, g_row, w_q, sub_keys, tm):
    n, d = y2d.shape
    n_heads, _, n_keys, half = sub_keys.shape
    assert n_keys == LANES and half == LANES
    sel = n_heads * PEER_TOPK
    return pl.pallas_call(
        _route_kernel,
        grid=(n // tm, n_heads),
        in_specs=[pl.BlockSpec((tm, d), lambda i, hd: (i, 0)),
                  pl.BlockSpec((1, d), lambda i, hd: (0, 0)),
                  pl.BlockSpec((d, 2 * half), lambda i, hd: (0, hd)),
                  pl.BlockSpec((1, 2, n_keys, half), lambda i, hd: (hd, 0, 0, 0))],
        out_specs=[pl.BlockSpec((tm, d), lambda i, hd: (i, 0)),
                   pl.BlockSpec((PEER_TOPK, tm), lambda i, hd: (hd, i)),
                   pl.BlockSpec((PEER_TOPK, tm), lambda i, hd: (hd, i))],
        out_shape=[jax.ShapeDtypeStruct((n, d), F32),
                   jax.ShapeDtypeStruct((sel, n), jnp.int32),
                   jax.ShapeDtypeStruct((sel, n), F32)],
        scratch_shapes=[pltpu.VMEM((tm, d), BF16)],
        compiler_params=_params(("parallel", "arbitrary")),
        name="route",
    )(y2d, g_row, w_q, sub_keys)


def _experts_kernel(idx_ref, h_ref, gate_ref, y_ref, uv_ref, o_ref, *scratch):
    tb, d = h_ref.shape
    sel = gate_ref.shape[0]
    bufs, sem = scratch[:-1], scratch[-1]
    nslot = len(bufs)
    ahead = nslot - 1

    def gather(t, slot):
        for j in range(sel):
            e = idx_ref[t * sel + j]
            pltpu.make_async_copy(uv_ref.at[e], bufs[slot].at[pl.ds(j, 1), :], sem.at[slot]).start()

    def wait(slot):
        pltpu.make_async_copy(bufs[slot], bufs[slot], sem.at[slot]).wait()

    for t0 in range(ahead):
        gather(t0, t0)
    lane = lax.broadcasted_iota(jnp.int32, gate_ref.shape, 1)
    half = d // 2

    def unpack(w):
        return (lax.bitcast_convert_type(w << 16, F32), lax.bitcast_convert_type(w & jnp.int32(-65536), F32))

    def group(g, carry):
        base = pl.multiple_of(g * SUBLANES, SUBLANES)
        x8 = h_ref[pl.ds(base, SUBLANES), :]
        rows = []
        for i in range(SUBLANES):
            t = base + i
            slot = i % nslot
            buf = bufs[slot]
            gather(jnp.minimum(t + ahead, tb - 1), (i + ahead) % nslot)
            wait(slot)
            acc = jnp.zeros((sel, LANES), F32)
            for c in range(half // LANES):
                lo, hi = unpack(buf[:, c * LANES:(c + 1) * LANES])
                acc = (acc + lo * x8[i:i + 1, c * LANES:(c + 1) * LANES]
                       + hi * x8[i:i + 1, half + c * LANES:half + (c + 1) * LANES])
            dots = jnp.sum(acc, axis=-1, keepdims=True)
            gate = jnp.sum(jnp.where(lane == t, gate_ref[...], 0.0), axis=-1, keepdims=True)
            act = jnp.broadcast_to(jax.nn.gelu(dots) * gate, (sel, LANES))
            y_lo, y_hi = [], []
            for c in range(half // LANES):
                lo, hi = unpack(buf[:, half + c * LANES:half + (c + 1) * LANES])
                y_lo.append(jnp.sum(lo * act, axis=0, keepdims=True))
                y_hi.append(jnp.sum(hi * act, axis=0, keepdims=True))
            rows.append(jnp.concatenate(y_lo + y_hi, axis=1))
        o_ref[pl.ds(base, SUBLANES), :] = y_ref[pl.ds(base, SUBLANES), :] + jnp.concatenate(rows, axis=0)
        return carry

    lax.fori_loop(0, tb // SUBLANES, group, 0)
    for extra in range(ahead):
        wait((tb + extra) % nslot)


def _experts(idx_flat, h2d, gate_t, y2d, uv, tb):
    n, d = h2d.shape
    sel = gate_t.shape[0]
    assert tb % SUBLANES == 0 and SUBLANES % EXPERT_SLOTS == 0
    return pl.pallas_call(
        _experts_kernel,
        grid=(n // tb,),
        in_specs=[pl.BlockSpec((tb * sel,), lambda i: (i,), memory_space=pltpu.SMEM),
                  pl.BlockSpec((tb, d), lambda i: (i, 0)),
                  pl.BlockSpec((sel, tb), lambda i: (0, i)),
                  pl.BlockSpec((tb, d), lambda i: (i, 0)),
                  pl.BlockSpec(memory_space=pl.ANY)],
        out_specs=pl.BlockSpec((tb, d), lambda i: (i, 0)),
        out_shape=jax.ShapeDtypeStruct((n, d), F32),
        scratch_shapes=[pltpu.VMEM((sel, d), jnp.int32)] * EXPERT_SLOTS + [pltpu.SemaphoreType.DMA((EXPERT_SLOTS,))],
        compiler_params=_params(("arbitrary",)),
        name="experts",
    )(idx_flat, h2d, gate_t, y2d, uv)


def _pack_bf16_halves(x):
    half = x.shape[1] // 2
    bits = lax.bitcast_convert_type(x.astype(BF16), jnp.uint16).astype(jnp.uint32)
    return lax.bitcast_convert_type(bits[:, :half] | (bits[:, half:] << 16), jnp.int32)


def _pick_tile(n, want):
    t = min(n, want)
    assert n % t == 0, (n, t)
    return t


def _layer_front(x2d, p):
    n, _ = x2d.shape
    tm = _pick_tile(n, 512)
    return _inproj(x2d, p["norm_mix_g"], p["w_main"], p["w_f_t"], p["b_f"], p["q_g"], p["k_g"], p["vs_g"], tm)


def _layer_back(x2d, fox_n, us, vs, p, chunk):
    n, _ = x2d.shape
    w_s = p["w_s"][:, :chunk, :chunk]
    b_s = p["b_s"][:, :chunk, None]
    y1 = _merge(x2d, fox_n, us, vs, w_s, b_s, p["g_sgu"], p["wo1"], p["wo2"], _pick_tile(n, 256))
    h2, idx_t, gate_t = _route(y1, p["norm_ffn_g"], p["w_q"], p["sub_keys"], _pick_tile(n, 256))
    idx_flat = idx_t.T.reshape(-1)
    return _experts(idx_flat, h2, gate_t, y1, p["uv"], _pick_tile(n, 128))


def kernel(x_prompt, x_sample, cache_fox_k, cache_fox_v, cache_fox_logf, norm_mix_g, w_in, b_f, q_norm_g, k_norm_g, sgu_v_norm_g, w_s, b_s, fox_out_norm_g, sgu_out_norm_g, w_out, norm_ffn_g, peer_w_q, peer_sub_keys, peer_u, peer_v):
    depth = w_in.shape[0]
    batch, seq, d = x_prompt.shape
    dec_batch, dec_seq, _ = x_sample.shape
    past = cache_fox_k.shape[2]
    nh, dh = cache_fox_k.shape[3], cache_fox_k.shape[4]
    ng, dg = sgu_v_norm_g.shape[1], sgu_v_norm_g.shape[2]
    assert dh == LANES and dg == LANES and nh == ng
    width = nh * dh
    assert w_in.shape[2] == 3 * width + nh + 2 * width

    y_p = x_prompt.reshape(batch * seq, d)
    y_s = x_sample.reshape(dec_batch * dec_seq, d)
    outs = [[] for _ in range(7)]
    for l in range(depth):
        wl = w_in[l]
        p = {
            "norm_mix_g": norm_mix_g[l][None, :],
            "w_main": jnp.concatenate([wl[:, :3 * width], wl[:, 3 * width + nh:]], axis=1).astype(BF16),
            "w_f_t": wl[:, 3 * width:3 * width + nh].T.astype(BF16),
            "b_f": b_f[l][:, None],
            "q_g": jnp.tile(q_norm_g[l], nh)[None, :],
            "k_g": jnp.tile(k_norm_g[l], nh)[None, :],
            "vs_g": sgu_v_norm_g[l].reshape(1, width),
            "w_s": w_s[l],
            "b_s": b_s[l],
            "g_sgu": sgu_out_norm_g[l].reshape(1, width),
            "wo1": w_out[l][:width].astype(BF16),
            "wo2": w_out[l][width:].astype(BF16),
            "norm_ffn_g": norm_ffn_g[l][None, :],
            "w_q": peer_w_q[l].astype(BF16),
            "sub_keys": peer_sub_keys[l].astype(BF16),
            "uv": jnp.concatenate([_pack_bf16_halves(peer_u[l]), _pack_bf16_halves(peer_v[l])], axis=1)[:, None, :],
        }
        g_fox = fox_out_norm_g[l]

        q, k, v, us, vs, lf_t = _layer_front(y_s, p)
        lf_new = lf_t.reshape(nh, dec_batch, dec_seq).transpose(1, 0, 2)
        lf_all = jnp.concatenate([cache_fox_logf[l].astype(F32).transpose(0, 2, 1), lf_new], axis=2)
        total = past + dec_seq
        padded = -(-total // LANES) * LANES
        lf_all = jnp.pad(lf_all, ((0, 0), (0, 0), (0, padded - total))).reshape(dec_batch * nh, padded)
        cum = _cumsum_rows(lf_all).reshape(dec_batch, nh, padded)
        ck_cache = cum[:, :, :past]
        ck_new = cum[:, :, past:total]
        cq_col = ck_new.transpose(0, 2, 1).reshape(dec_batch * dec_seq, nh)
        fox_n = _attn_sample(q, k, v, cache_fox_k[l].reshape(dec_batch, past, width),
                             cache_fox_v[l].reshape(dec_batch, past, width),
                             cq_col, ck_cache, ck_new, g_fox.reshape(1, width), dec_seq)
        y_s = _layer_back(y_s, fox_n, us, vs, p, dec_seq)
        outs[3].append(k.reshape(dec_batch, dec_seq, nh, dh))
        outs[4].append(v.reshape(dec_batch, dec_seq, nh, dh))
        outs[5].append(lf_t.T.reshape(dec_batch, dec_seq, nh))
        outs[6].append(vs.reshape(dec_batch, dec_seq, ng, dg))

        q, k, v, us, vs, lf_t = _layer_front(y_p, p)
        lf_rows = lf_t.reshape(nh, batch, seq).transpose(1, 0, 2).reshape(batch * nh, seq)
        cum = _cumsum_rows(lf_rows)
        cq_col = cum.reshape(batch, nh, seq).transpose(0, 2, 1).reshape(batch * seq, nh)
        fox_n = _attn_prompt(q, k, v, cq_col, g_fox[:, None, :], batch, seq, _pick_tile(seq, 512))
        y_p = _layer_back(y_p, fox_n, us, vs, p, SGU_CHUNK)
        outs[0].append(k.reshape(batch, seq, nh, dh))
        outs[1].append(v.reshape(batch, seq, nh, dh))
        outs[2].append(lf_t.T.reshape(batch, seq, nh))

    return (y_p.reshape(batch, seq, d), y_s.reshape(dec_batch, dec_seq, d)) + tuple(jnp.stack(o) for o in outs)
```

```python
import functools

import jax
import jax.numpy as jnp
from jax import lax
from jax.experimental import pallas as pl
from jax.experimental.pallas import tpu as pltpu

EPS = 1e-6
NEG_INF = -1e30
LANES = 128
SUBLANES = 8
PEER_TOPK = 16
SGU_CHUNK = 128
EXPERT_SLOTS = 4
VMEM_LIMIT = 52 * 1024 * 1024

F32 = jnp.float32
BF16 = jnp.bfloat16


def _params(sem):
    return pltpu.CompilerParams(dimension_semantics=sem, vmem_limit_bytes=VMEM_LIMIT)


def _rms(x, g):
    return x * lax.rsqrt(jnp.mean(x * x, axis=-1, keepdims=True) + EPS) * g


def _log_sigmoid(x):
    return jnp.minimum(x, 0.0) - jnp.log1p(jnp.exp(-jnp.abs(x)))


def _inproj_kernel(x_ref, g_ref, w_ref, wf_ref, bf_ref, qg_ref, kg_ref, vg_ref,
                   q_ref, k_ref, v_ref, us_ref, vs_ref, lf_ref, h_scr):
    s = pl.program_id(1)
    width = w_ref.shape[1]

    @pl.when(s == 0)
    def _():
        hb = _rms(x_ref[...], g_ref[...]).astype(BF16)
        h_scr[...] = hb
        fl = lax.dot_general(wf_ref[...], hb, (((1,), (1,)), ((), ())), preferred_element_type=F32)
        lf_ref[...] = _log_sigmoid(fl + bf_ref[...])

    z = jnp.dot(h_scr[...], w_ref[...], preferred_element_type=F32)

    def headwise(o_ref, fn):
        for h in range(width // LANES):
            sl = slice(h * LANES, (h + 1) * LANES)
            o_ref[:, sl] = fn(z[:, sl], sl)

    @pl.when(s == 0)
    def _():
        headwise(q_ref, lambda zh, sl: _rms(zh, qg_ref[:, sl]))

    @pl.when(s == 1)
    def _():
        headwise(k_ref, lambda zh, sl: _rms(zh, kg_ref[:, sl]))

    @pl.when(s == 2)
    def _():
        v_ref[...] = z

    @pl.when(s == 3)
    def _():
        us_ref[...] = jax.nn.gelu(z)

    @pl.when(s == 4)
    def _():
        headwise(vs_ref, lambda zh, sl: _rms(jax.nn.gelu(zh), vg_ref[:, sl]))


def _inproj(x2d, norm_g, w_main, w_f_t, b_f, q_g, k_g, vs_g, tm):
    n, d = x2d.shape
    width = w_main.shape[1] // 5
    nh = w_f_t.shape[0]
    full = lambda shape: pl.BlockSpec(shape, lambda i, s: (0,) * len(shape))
    tok = pl.BlockSpec((tm, width), lambda i, s: (i, 0))
    out_sds = jax.ShapeDtypeStruct((n, width), F32)
    return pl.pallas_call(
        _inproj_kernel,
        grid=(n // tm, 5),
        in_specs=[pl.BlockSpec((tm, d), lambda i, s: (i, 0)), full((1, d)),
                  pl.BlockSpec((d, width), lambda i, s: (0, s)), full((nh, d)), full((nh, 1)),
                  full((1, width)), full((1, width)), full((1, width))],
        out_specs=[tok, tok, tok, tok, tok, pl.BlockSpec((nh, tm), lambda i, s: (0, i))],
        out_shape=[out_sds] * 5 + [jax.ShapeDtypeStruct((nh, n), F32)],
        scratch_shapes=[pltpu.VMEM((tm, d), BF16)],
        compiler_params=_params(("parallel", "arbitrary")),
        name="inproj",
    )(x2d, norm_g, w_main, w_f_t, b_f, q_g, k_g, vs_g)


def _cumsum_kernel(x_ref, o_ref):
    rows, length = x_ref.shape
    r = lax.broadcasted_iota(jnp.int32, (LANES, LANES), 0)
    c = lax.broadcasted_iota(jnp.int32, (LANES, LANES), 1)
    tri = jnp.where(r <= c, 1.0, 0.0).astype(BF16)
    carry = jnp.zeros((rows, 1), F32)
    for j in range(length // LANES):
        sl = slice(j * LANES, (j + 1) * LANES)
        x = x_ref[:, sl]
        hi = x.astype(BF16)
        r1 = x - hi.astype(F32)
        mid = r1.astype(BF16)
        lo = (r1 - mid.astype(F32)).astype(BF16)
        cs = (jnp.dot(hi, tri, preferred_element_type=F32)
              + jnp.dot(mid, tri, preferred_element_type=F32)
              + jnp.dot(lo, tri, preferred_element_type=F32))
        out = cs + carry
        o_ref[:, sl] = out
        carry = out[:, LANES - 1:LANES]


def _cumsum_rows(x):
    rows, length = x.shape
    rb = _pick_tile(rows, 8)
    return pl.pallas_call(
        _cumsum_kernel,
        grid=(rows // rb,),
        in_specs=[pl.BlockSpec((rb, length), lambda i: (i, 0))],
        out_specs=pl.BlockSpec((rb, length), lambda i: (i, 0)),
        out_shape=jax.ShapeDtypeStruct((rows, length), F32),
        compiler_params=_params(("parallel",)),
        name="cumsum",
    )(x)


def _split3(x):
    hi = x.astype(BF16).astype(F32)
    r = x - hi
    mid = r.astype(BF16).astype(F32)
    return hi, mid, (r - mid).astype(BF16).astype(F32)


def _decay_lanes(c, query_side):
    rows = c.shape[0]
    lane = lax.broadcasted_iota(jnp.int32, (rows, LANES), 1)
    hi, mid, lo = _split3(c if query_side else -c)
    if query_side:
        out = jnp.where(lane == 0, hi, jnp.where(lane == 1, mid, jnp.where(lane == 2, lo,
                                                                          jnp.where(lane < 6, 1.0, 0.0))))
    else:
        out = jnp.where(lane < 3, 1.0, jnp.where(lane == 3, hi, jnp.where(lane == 4, mid,
                                                                         jnp.where(lane == 5, lo, 0.0))))
    return out.astype(BF16)


def _attn_prompt_kernel(q_ref, k_ref, v_ref, cum_ref, g_ref, o_ref, qa_scr, ka_scr, va_scr, m_scr, acc_scr, *, scale):
    h = pl.program_id(1)
    qi = pl.program_id(2)
    tq = q_ref.shape[0]
    seq = k_ref.shape[0]
    nh = cum_ref.shape[1]
    c1 = scale * 1.4426950408889634
    head_lane = lax.broadcasted_iota(jnp.int32, (tq, nh), 1)

    def head_col(x):
        return jnp.sum(jnp.where(head_lane == h, x, 0.0), axis=-1, keepdims=True) * (1.0 / scale)

    @pl.when(qi == 0)
    def _():
        for j in range(seq // tq):
            rows = slice(j * tq, (j + 1) * tq)
            ka_scr[rows, :LANES] = k_ref[rows, :].astype(BF16)
            ka_scr[rows, LANES:] = _decay_lanes(head_col(cum_ref[rows, :]), False)
            va_scr[rows, :LANES] = v_ref[rows, :].astype(BF16)
            va_scr[rows, LANES:] = jnp.ones((tq, LANES), BF16)

    q0 = pl.multiple_of(qi * tq, tq)
    qa_scr[:, :LANES] = q_ref[...].astype(BF16)
    qa_scr[:, LANES:] = _decay_lanes(head_col(cum_ref[pl.ds(q0, tq), :]), True)
    m_scr[...] = jnp.full(m_scr.shape, NEG_INF, F32)
    acc_scr[...] = jnp.zeros(acc_scr.shape, F32)

    def tile(ki, masked):
        k0 = pl.multiple_of(ki * tq, tq)
        s = lax.dot_general(qa_scr[...], ka_scr[pl.ds(k0, tq), :], (((1,), (1,)), ((), ())),
                            preferred_element_type=F32)
        if masked:
            qpos = lax.broadcasted_iota(jnp.int32, (tq, tq), 0)
            kpos = lax.broadcasted_iota(jnp.int32, (tq, tq), 1)
            s = jnp.where(kpos <= qpos, s, NEG_INF)
        m_old = m_scr[...]
        m_new = jnp.maximum(m_old, jnp.max(s, axis=-1, keepdims=True))
        p = jnp.exp2(s * c1 - m_new * c1)
        alpha = jnp.exp2((m_old - m_new) * c1)
        acc_scr[...] = alpha * acc_scr[...] + jnp.dot(p.astype(BF16), va_scr[pl.ds(k0, tq), :],
                                                      preferred_element_type=F32)
        m_scr[...] = m_new

    def body(ki, carry):
        tile(ki, False)
        return carry

    lax.fori_loop(0, qi, body, 0)
    tile(qi, True)
    o_ref[...] = _rms(acc_scr[:, :LANES] / acc_scr[:, LANES:], g_ref[0])


def _attn_prompt(q, k, v, cum_col, g_fox, batch, seq, tq):
    n, width = q.shape
    nh = width // LANES
    nt = seq // tq
    scale = float(LANES) ** -0.5
    return pl.pallas_call(
        functools.partial(_attn_prompt_kernel, scale=scale),
        grid=(batch, nh, nt),
        in_specs=[pl.BlockSpec((tq, LANES), lambda b, h, qi: (b * nt + qi, h)),
                  pl.BlockSpec((seq, LANES), lambda b, h, qi: (b, h)),
                  pl.BlockSpec((seq, LANES), lambda b, h, qi: (b, h)),
                  pl.BlockSpec((seq, nh), lambda b, h, qi: (b, 0)),
                  pl.BlockSpec((1, 1, LANES), lambda b, h, qi: (h, 0, 0))],
        out_specs=pl.BlockSpec((tq, LANES), lambda b, h, qi: (b * nt + qi, h)),
        out_shape=jax.ShapeDtypeStruct((n, width), F32),
        scratch_shapes=[pltpu.VMEM((tq, 2 * LANES), BF16), pltpu.VMEM((seq, 2 * LANES), BF16),
                        pltpu.VMEM((seq, 2 * LANES), BF16), pltpu.VMEM((tq, 1), F32),
                        pltpu.VMEM((tq, 2 * LANES), F32)],
        compiler_params=_params(("parallel", "parallel", "arbitrary")),
        name="attn_prompt",
    )(q, k, v, cum_col, g_fox)


def _attn_sample_kernel(q_ref, kn_ref, vn_ref, kc_ref, vc_ref, cq_ref, ckc_ref, ckn_ref, g_ref, o_ref, *, scale):
    ts, nh = cq_ref.shape
    dn = (((1,), (1,)), ((), ()))
    qpos = lax.broadcasted_iota(jnp.int32, (ts, ts), 0)
    kpos = lax.broadcasted_iota(jnp.int32, (ts, ts), 1)
    for h in range(nh):
        sl = slice(h * LANES, (h + 1) * LANES)
        q = q_ref[:, sl].astype(BF16)
        cq = cq_ref[:, h:h + 1]
        s1 = lax.dot_general(q, kc_ref[:, h, :].astype(BF16), dn, preferred_element_type=F32) * scale
        s1 = s1 + (cq - ckc_ref[h])
        s2 = lax.dot_general(q, kn_ref[:, sl].astype(BF16), dn, preferred_element_type=F32) * scale
        s2 = jnp.where(kpos <= qpos, s2 + (cq - ckn_ref[h]), NEG_INF)
        m = jnp.maximum(jnp.max(s1, axis=-1, keepdims=True), jnp.max(s2, axis=-1, keepdims=True))
        p1 = jnp.exp(s1 - m)
        p2 = jnp.exp(s2 - m)
        l = jnp.sum(p1, axis=-1, keepdims=True) + jnp.sum(p2, axis=-1, keepdims=True)
        o = (jnp.dot(p1.astype(BF16), vc_ref[:, h, :].astype(BF16), preferred_element_type=F32)
             + jnp.dot(p2.astype(BF16), vn_ref[:, sl].astype(BF16), preferred_element_type=F32)) / l
        o_ref[:, sl] = _rms(o, g_ref[:, sl])


def _attn_sample(q, k_new, v_new, k_cache, v_cache, cq_col, ck_cache, ck_new, g_fox_row, ts):
    n, width = q.shape
    nb, past, nh, _ = k_cache.shape
    scale = float(LANES) ** -0.5
    tok = pl.BlockSpec((ts, width), lambda b: (b, 0))
    cache = pl.BlockSpec((None, past, nh, LANES), lambda b: (b, 0, 0, 0))
    return pl.pallas_call(
        functools.partial(_attn_sample_kernel, scale=scale),
        grid=(nb,),
        in_specs=[tok, tok, tok, cache, cache,
                  pl.BlockSpec((ts, nh), lambda b: (b, 0)),
                  pl.BlockSpec((nh, 1, past), lambda b: (b, 0, 0)),
                  pl.BlockSpec((nh, 1, ts), lambda b: (b, 0, 0)),
                  pl.BlockSpec((1, width), lambda b: (0, 0))],
        out_specs=tok,
        out_shape=jax.ShapeDtypeStruct((n, width), F32),
        compiler_params=_params(("parallel",)),
        name="attn_sample",
    )(q, k_new, v_new, k_cache, v_cache, cq_col, ck_cache, ck_new, g_fox_row)


def _merge_kernel(x_ref, fox_ref, us_ref, vs_ref, ws_ref, bs_ref, gs_ref, wo1_ref, wo2_ref, o_ref, cat_scr):
    tm = x_ref.shape[0]
    ng, chunk, _ = ws_ref.shape
    r = lax.broadcasted_iota(jnp.int32, (chunk, chunk), 0)
    c = lax.broadcasted_iota(jnp.int32, (chunk, chunk), 1)
    for g in range(ng):
        sl = slice(g * LANES, (g + 1) * LANES)
        wg = jnp.where(c <= r, ws_ref[g], 0.0).astype(BF16)
        bias = bs_ref[g]
        for j in range(tm // chunk):
            rows = slice(j * chunk, (j + 1) * chunk)
            mixed = jnp.dot(wg, vs_ref[rows, sl].astype(BF16), preferred_element_type=F32) + bias
            cat_scr[rows, sl] = _rms(us_ref[rows, sl] * mixed, gs_ref[:, sl]).astype(BF16)
    o_ref[...] = (x_ref[...]
                  + jnp.dot(fox_ref[...].astype(BF16), wo1_ref[...], preferred_element_type=F32)
                  + jnp.dot(cat_scr[...], wo2_ref[...], preferred_element_type=F32))


def _merge(x2d, fox_n, us, vs, w_s, b_s_col, g_sgu_row, wo1, wo2, tm):
    n, d = x2d.shape
    width = fox_n.shape[1]
    ng, chunk, _ = w_s.shape
    full = lambda shape: pl.BlockSpec(shape, lambda i: (0,) * len(shape))
    tok = pl.BlockSpec((tm, width), lambda i: (i, 0))
    row = pl.BlockSpec((tm, d), lambda i: (i, 0))
    return pl.pallas_call(
        _merge_kernel,
        grid=(n // tm,),
        in_specs=[row, tok, tok, tok, full((ng, chunk, chunk)), full((ng, chunk, 1)), full((1, width)),
                  full((width, d)), full((width, d))],
        out_specs=row,
        out_shape=jax.ShapeDtypeStruct((n, d), F32),
        scratch_shapes=[pltpu.VMEM((tm, width), BF16)],
        compiler_params=_params(("parallel",)),
        name="merge",
    )(x2d, fox_n, us, vs, w_s, b_s_col, g_sgu_row, wo1, wo2)


def _top_rows(s, k, ids=None, payload=None):
    if ids is None:
        ids = lax.broadcasted_iota(jnp.int32, s.shape, 0)
    big = jnp.int32(2 ** 30)
    vals, picks = [], []
    for _ in range(k):
        m = jnp.max(s, axis=0, keepdims=True)
        i = jnp.min(jnp.where(s == m, ids, big), axis=0, keepdims=True)
        hit = ids == i
        vals.append(m)
        picks.append(i if payload is None else jnp.max(jnp.where(hit, payload, -1), axis=0, keepdims=True))
        s = jnp.where(hit, -jnp.inf, s)
    return jnp.concatenate(vals, axis=0), jnp.concatenate(picks, axis=0)


def _stair_blocks(kk):
    need = {(a, j) for a in range(kk) for j in range(kk) if (a + 1) * (j + 1) <= kk}
    blocks = []
    for a in range(kk):
        nj = kk // (a + 1)
        if nj >= SUBLANES:
            for j0 in range(0, nj, SUBLANES):
                blocks.append(("row", a, j0, 0, min(SUBLANES, nj - j0)))
                need -= {(a, j0 + r) for r in range(SUBLANES)}
    for j in range(kk):
        for a0 in range(0, kk, SUBLANES):
            rs = [r for r in range(SUBLANES) if (a0 + r, j) in need]
            if rs:
                assert rs == list(range(rs[0], rs[-1] + 1))
                blocks.append(("col", a0, j, rs[0], rs[-1] + 1))
                need -= {(a0 + r, j) for r in rs}
    assert not need
    return blocks


def _pair_candidates(sv, si, n_keys):
    kk, cols = sv[0].shape
    sub = lax.broadcasted_iota(jnp.int32, (SUBLANES, cols), 0)
    vals, flat, expert = [], [], []
    for kind, a0, j0, lo, hi in _stair_blocks(kk):
        if kind == "row":
            v = sv[0][a0:a0 + 1] + sv[1][j0:j0 + SUBLANES]
            e = si[0][a0:a0 + 1] * n_keys + si[1][j0:j0 + SUBLANES]
            f = a0 * kk + j0 + sub
        else:
            v = sv[0][a0:a0 + SUBLANES] + sv[1][j0:j0 + 1]
            e = si[0][a0:a0 + SUBLANES] * n_keys + si[1][j0:j0 + 1]
            f = (a0 + sub) * kk + j0
        if (lo, hi) != (0, SUBLANES):
            v = jnp.where(sub < lo, -jnp.inf, jnp.where(sub < hi, v, -jnp.inf))
        vals.append(v)
        flat.append(f)
        expert.append(e)
    return jnp.concatenate(vals, axis=0), jnp.concatenate(flat, axis=0), jnp.concatenate(expert, axis=0)


def _route_kernel(y_ref, g_ref, wq_ref, sk_ref, h_ref, idx_ref, gate_ref, hb_scr):
    n_keys = sk_ref.shape[2]
    kk = PEER_TOPK

    @pl.when(pl.program_id(1) == 0)
    def _():
        h = _rms(y_ref[...], g_ref[...])
        h_ref[...] = h
        hb_scr[...] = h.astype(BF16)

    q = jnp.dot(hb_scr[...], wq_ref[...], preferred_element_type=F32).astype(BF16)
    dn = (((1,), (1,)), ((), ()))
    sv, si = [], []
    for c in range(2):
        st = lax.dot_general(sk_ref[0, c], q[:, c * LANES:(c + 1) * LANES], dn, preferred_element_type=F32)
        v, i = _top_rows(st, kk)
        sv.append(v)
        si.append(i)
    cand, flat, expert = _pair_candidates(sv, si, n_keys)
    top_s, top_i = _top_rows(cand, kk, ids=flat, payload=expert)
    e = jnp.exp(top_s - top_s[0:1])
    gate_ref[...] = e / jnp.sum(e, axis=0, keepdims=True)
    idx_ref[...] = top_i


def _route(y2d, g_row, w_q, sub_keys, tm):
    n, d = y2d.shape
    n_heads, _, n_keys, half = sub_keys.shape
    assert n_keys == LANES and half == LANES
    sel = n_heads * PEER_TOPK
    return pl.pallas_call(
        _route_kernel,
        grid=(n // tm, n_heads),
        in_specs=[pl.BlockSpec((tm, d), lambda i, hd: (i, 0)),
                  pl.BlockSpec((1, d), lambda i, hd: (0, 0)),
                  pl.BlockSpec((d, 2 * half), lambda i, hd: (0, hd)),
                  pl.BlockSpec((1, 2, n_keys, half), lambda i, hd: (hd, 0, 0, 0))],
        out_specs=[pl.BlockSpec((tm, d), lambda i, hd: (i, 0)),
                   pl.BlockSpec((PEER_TOPK, tm), lambda i, hd: (hd, i)),
                   pl.BlockSpec((PEER_TOPK, tm), lambda i, hd: (hd, i))],
        out_shape=[jax.ShapeDtypeStruct((n, d), F32),
                   jax.ShapeDtypeStruct((sel, n), jnp.int32),
                   jax.ShapeDtypeStruct((sel, n), F32)],
        scratch_shapes=[pltpu.VMEM((tm, d), BF16)],
        compiler_params=_params(("parallel", "arbitrary")),
        name="route",
    )(y2d, g_row, w_q, sub_keys)


def _experts_kernel(idx_ref, h_ref, gate_ref, y_ref, uv_ref, o_ref, *scratch):
    tb, d = h_ref.shape
    sel = gate_ref.shape[0]
    bufs, sem = scratch[:-1], scratch[-1]
    nslot = len(bufs)
    ahead = nslot - 1

    def gather(t, slot):
        for j in range(sel):
            e = idx_ref[t * sel + j]
            pltpu.make_async_copy(uv_ref.at[e], bufs[slot].at[pl.ds(j, 1), :], sem.at[slot]).start(priority=j % 2)

    def wait(slot):
        pltpu.make_async_copy(bufs[slot], bufs[slot], sem.at[slot]).wait()

    for t0 in range(ahead):
        gather(t0, t0)
    lane = lax.broadcasted_iota(jnp.int32, gate_ref.shape, 1)
    half = d // 2

    def unpack(w):
        return (lax.bitcast_convert_type(w << 16, F32), lax.bitcast_convert_type(w & jnp.int32(-65536), F32))

    def group(g, carry):
        base = pl.multiple_of(g * SUBLANES, SUBLANES)
        x8 = h_ref[pl.ds(base, SUBLANES), :]
        rows = []
        for i in range(SUBLANES):
            t = base + i
            slot = i % nslot
            buf = bufs[slot]
            gather(jnp.minimum(t + ahead, tb - 1), (i + ahead) % nslot)
            wait(slot)
            acc = jnp.zeros((sel, LANES), F32)
            for c in range(half // LANES):
                lo, hi = unpack(buf[:, c * LANES:(c + 1) * LANES])
                acc = (acc + lo * x8[i:i + 1, c * LANES:(c + 1) * LANES]
                       + hi * x8[i:i + 1, half + c * LANES:half + (c + 1) * LANES])
            dots = jnp.sum(acc, axis=-1, keepdims=True)
            gate = jnp.sum(jnp.where(lane == t, gate_ref[...], 0.0), axis=-1, keepdims=True)
            act = jnp.broadcast_to(jax.nn.gelu(dots) * gate, (sel, LANES))
            y_lo, y_hi = [], []
            for c in range(half // LANES):
                lo, hi = unpack(buf[:, half + c * LANES:half + (c + 1) * LANES])
                y_lo.append(jnp.sum(lo * act, axis=0, keepdims=True))
                y_hi.append(jnp.sum(hi * act, axis=0, keepdims=True))
            rows.append(jnp.concatenate(y_lo + y_hi, axis=1))
        o_ref[pl.ds(base, SUBLANES), :] = y_ref[pl.ds(base, SUBLANES), :] + jnp.concatenate(rows, axis=0)
        return carry

    lax.fori_loop(0, tb // SUBLANES, group, 0)
    for extra in range(ahead):
        wait((tb + extra) % nslot)


def _experts(idx_flat, h2d, gate_t, y2d, uv, tb):
    n, d = h2d.shape
    sel = gate_t.shape[0]
    assert tb % SUBLANES == 0 and SUBLANES % EXPERT_SLOTS == 0
    return pl.pallas_call(
        _experts_kernel,
        grid=(n // tb,),
        in_specs=[pl.BlockSpec((tb * sel,), lambda i: (i,), memory_space=pltpu.SMEM),
                  pl.BlockSpec((tb, d), lambda i: (i, 0)),
                  pl.BlockSpec((sel, tb), lambda i: (0, i)),
                  pl.BlockSpec((tb, d), lambda i: (i, 0)),
                  pl.BlockSpec(memory_space=pl.ANY)],
        out_specs=pl.BlockSpec((tb, d), lambda i: (i, 0)),
        out_shape=jax.ShapeDtypeStruct((n, d), F32),
        scratch_shapes=[pltpu.VMEM((sel, d), jnp.int32)] * EXPERT_SLOTS + [pltpu.SemaphoreType.DMA((EXPERT_SLOTS,))],
        compiler_params=_params(("arbitrary",)),
        name="experts",
    )(idx_flat, h2d, gate_t, y2d, uv)


def _pack_bf16_halves(x):
    half = x.shape[1] // 2
    bits = lax.bitcast_convert_type(lax.reduce_precision(x, exponent_bits=8, mantissa_bits=7), jnp.uint32)
    return lax.bitcast_convert_type((bits[:, :half] >> 16) | bits[:, half:], jnp.int32)


def _pick_tile(n, want):
    t = min(n, want)
    assert n % t == 0, (n, t)
    return t


def _layer_front(x2d, p):
    n, _ = x2d.shape
    tm = _pick_tile(n, 512)
    return _inproj(x2d, p["norm_mix_g"], p["w_main"], p["w_f_t"], p["b_f"], p["q_g"], p["k_g"], p["vs_g"], tm)


def _layer_back(x2d, fox_n, us, vs, p, chunk):
    n, _ = x2d.shape
    w_s = p["w_s"][:, :chunk, :chunk]
    b_s = p["b_s"][:, :chunk, None]
    y1 = _merge(x2d, fox_n, us, vs, w_s, b_s, p["g_sgu"], p["wo1"], p["wo2"], _pick_tile(n, 256))
    h2, idx_t, gate_t = _route(y1, p["norm_ffn_g"], p["w_q"], p["sub_keys"], _pick_tile(n, 256))
    idx_flat = idx_t.T.reshape(-1)
    return _experts(idx_flat, h2, gate_t, y1, p["uv"], _pick_tile(n, 128))


def kernel(x_prompt, x_sample, cache_fox_k, cache_fox_v, cache_fox_logf, norm_mix_g, w_in, b_f, q_norm_g, k_norm_g, sgu_v_norm_g, w_s, b_s, fox_out_norm_g, sgu_out_norm_g, w_out, norm_ffn_g, peer_w_q, peer_sub_keys, peer_u, peer_v):
    depth = w_in.shape[0]
    batch, seq, d = x_prompt.shape
    dec_batch, dec_seq, _ = x_sample.shape
    past = cache_fox_k.shape[2]
    nh, dh = cache_fox_k.shape[3], cache_fox_k.shape[4]
    ng, dg = sgu_v_norm_g.shape[1], sgu_v_norm_g.shape[2]
    assert dh == LANES and dg == LANES and nh == ng
    width = nh * dh
    assert w_in.shape[2] == 3 * width + nh + 2 * width

    y_p = x_prompt.reshape(batch * seq, d)
    y_s = x_sample.reshape(dec_batch * dec_seq, d)
    outs = [[] for _ in range(7)]
    for l in range(depth):
        wl = w_in[l]
        p = {
            "norm_mix_g": norm_mix_g[l][None, :],
            "w_main": jnp.concatenate([wl[:, :3 * width], wl[:, 3 * width + nh:]], axis=1).astype(BF16),
            "w_f_t": wl[:, 3 * width:3 * width + nh].T.astype(BF16),
            "b_f": b_f[l][:, None],
            "q_g": jnp.tile(q_norm_g[l], nh)[None, :],
            "k_g": jnp.tile(k_norm_g[l], nh)[None, :],
            "vs_g": sgu_v_norm_g[l].reshape(1, width),
            "w_s": w_s[l],
            "b_s": b_s[l],
            "g_sgu": sgu_out_norm_g[l].reshape(1, width),
            "wo1": w_out[l][:width].astype(BF16),
            "wo2": w_out[l][width:].astype(BF16),
            "norm_ffn_g": norm_ffn_g[l][None, :],
            "w_q": peer_w_q[l].astype(BF16),
            "sub_keys": peer_sub_keys[l].astype(BF16),
            "uv": jnp.concatenate([_pack_bf16_halves(peer_u[l]), _pack_bf16_halves(peer_v[l])], axis=1)[:, None, :],
        }
        g_fox = fox_out_norm_g[l]

        q, k, v, us, vs, lf_t = _layer_front(y_s, p)
        lf_new = lf_t.reshape(nh, dec_batch, dec_seq).transpose(1, 0, 2)
        lf_all = jnp.concatenate([cache_fox_logf[l].astype(F32).transpose(0, 2, 1), lf_new], axis=2)
        total = past + dec_seq
        padded = -(-total // LANES) * LANES
        lf_all = jnp.pad(lf_all, ((0, 0), (0, 0), (0, padded - total))).reshape(dec_batch * nh, padded)
        cum = _cumsum_rows(lf_all).reshape(dec_batch, nh, padded)
        ck_cache = cum[:, :, :past].reshape(dec_batch * nh, 1, past)
        ck_new = cum[:, :, past:total]
        cq_col = ck_new.transpose(0, 2, 1).reshape(dec_batch * dec_seq, nh)
        fox_n = _attn_sample(q, k, v, cache_fox_k[l], cache_fox_v[l], cq_col, ck_cache,
                             ck_new.reshape(dec_batch * nh, 1, dec_seq), g_fox.reshape(1, width), dec_seq)
        y_s = _layer_back(y_s, fox_n, us, vs, p, dec_seq)
        outs[3].append(k.reshape(dec_batch, dec_seq, nh, dh))
        outs[4].append(v.reshape(dec_batch, dec_seq, nh, dh))
        outs[5].append(lf_t.T.reshape(dec_batch, dec_seq, nh))
        outs[6].append(vs.reshape(dec_batch, dec_seq, ng, dg))

        q, k, v, us, vs, lf_t = _layer_front(y_p, p)
        lf_rows = lf_t.reshape(nh, batch, seq).transpose(1, 0, 2).reshape(batch * nh, seq)
        cum = _cumsum_rows(lf_rows)
        cq_col = cum.reshape(batch, nh, seq).transpose(0, 2, 1).reshape(batch * seq, nh)
        fox_n = _attn_prompt(q, k, v, cq_col, g_fox[:, None, :], batch, seq, _pick_tile(seq, 512))
        y_p = _layer_back(y_p, fox_n, us, vs, p, SGU_CHUNK)
        outs[0].append(k.reshape(batch, seq, nh, dh))
        outs[1].append(v.reshape(batch, seq, nh, dh))
        outs[2].append(lf_t.T.reshape(batch, seq, nh))

    return (y_p.reshape(batch, seq, d), y_s.reshape(dec_batch, dec_seq, d)) + tuple(jnp.stack(o) for o in outs)
```

```python
import functools

import jax
import jax.numpy as jnp
from jax import lax
from jax.experimental import pallas as pl
from jax.experimental.pallas import tpu as pltpu

EPS = 1e-6
NEG_INF = -1e30
LANES = 128
SUBLANES = 8
MXU_COLS = 256
PEER_TOPK = 16
SGU_CHUNK = 128
EXPERT_SLOTS = 4
VMEM_LIMIT = 52 * 1024 * 1024

F32 = jnp.float32
BF16 = jnp.bfloat16


def _params(sem):
    return pltpu.CompilerParams(dimension_semantics=sem, vmem_limit_bytes=VMEM_LIMIT)


def _rms(x, g):
    return x * lax.rsqrt(jnp.mean(x * x, axis=-1, keepdims=True) + EPS) * g


def _log_sigmoid(x):
    return jnp.minimum(x, 0.0) - jnp.log1p(jnp.exp(-jnp.abs(x)))


def _inproj_kernel(x_ref, g_ref, w_ref, wf_ref, bf_ref, qg_ref, kg_ref, vg_ref,
                   q_ref, k_ref, v_ref, us_ref, vs_ref, lf_ref, h_scr):
    s = pl.program_id(1)
    width = w_ref.shape[1]

    @pl.when(s == 0)
    def _():
        hb = _rms(x_ref[...], g_ref[...]).astype(BF16)
        h_scr[...] = hb
        fl = lax.dot_general(wf_ref[...], hb, (((1,), (1,)), ((), ())), preferred_element_type=F32)
        lf_ref[...] = _log_sigmoid(fl + bf_ref[...])

    def headwise(o_ref, fn):
        for c0 in range(0, width, MXU_COLS):
            z = jnp.dot(h_scr[...], w_ref[:, c0:c0 + MXU_COLS], preferred_element_type=F32)
            for h0 in range(0, MXU_COLS, LANES):
                sl = slice(c0 + h0, c0 + h0 + LANES)
                o_ref[:, sl] = fn(z[:, h0:h0 + LANES], sl)

    @pl.when(s == 0)
    def _():
        headwise(q_ref, lambda zh, sl: _rms(zh, qg_ref[:, sl]))

    @pl.when(s == 1)
    def _():
        headwise(k_ref, lambda zh, sl: _rms(zh, kg_ref[:, sl]))

    @pl.when(s == 2)
    def _():
        headwise(v_ref, lambda zh, sl: zh)

    @pl.when(s == 3)
    def _():
        headwise(us_ref, lambda zh, sl: jax.nn.gelu(zh))

    @pl.when(s == 4)
    def _():
        headwise(vs_ref, lambda zh, sl: _rms(jax.nn.gelu(zh), vg_ref[:, sl]))


def _inproj(x2d, norm_g, w_main, w_f_t, b_f, q_g, k_g, vs_g, tm):
    n, d = x2d.shape
    width = w_main.shape[1] // 5
    nh = w_f_t.shape[0]
    full = lambda shape: pl.BlockSpec(shape, lambda i, s: (0,) * len(shape))
    tok = pl.BlockSpec((tm, width), lambda i, s: (i, 0))
    out_sds = jax.ShapeDtypeStruct((n, width), F32)
    return pl.pallas_call(
        _inproj_kernel,
        grid=(n // tm, 5),
        in_specs=[pl.BlockSpec((tm, d), lambda i, s: (i, 0)), full((1, d)),
                  pl.BlockSpec((d, width), lambda i, s: (0, s)), full((nh, d)), full((nh, 1)),
                  full((1, width)), full((1, width)), full((1, width))],
        out_specs=[tok, tok, tok, tok, tok, pl.BlockSpec((nh, tm), lambda i, s: (0, i))],
        out_shape=[out_sds] * 5 + [jax.ShapeDtypeStruct((nh, n), F32)],
        scratch_shapes=[pltpu.VMEM((tm, d), BF16)],
        compiler_params=_params(("parallel", "arbitrary")),
        name="inproj",
    )(x2d, norm_g, w_main, w_f_t, b_f, q_g, k_g, vs_g)


def _cumsum_kernel(x_ref, o_ref):
    rows, length = x_ref.shape
    r = lax.broadcasted_iota(jnp.int32, (LANES, LANES), 0)
    c = lax.broadcasted_iota(jnp.int32, (LANES, LANES), 1)
    tri = jnp.where(r <= c, 1.0, 0.0).astype(BF16)
    carry = jnp.zeros((rows, 1), F32)
    for j in range(length // LANES):
        sl = slice(j * LANES, (j + 1) * LANES)
        x = x_ref[:, sl]
        hi = x.astype(BF16)
        r1 = x - hi.astype(F32)
        mid = r1.astype(BF16)
        lo = (r1 - mid.astype(F32)).astype(BF16)
        cs = (jnp.dot(hi, tri, preferred_element_type=F32)
              + jnp.dot(mid, tri, preferred_element_type=F32)
              + jnp.dot(lo, tri, preferred_element_type=F32))
        out = cs + carry
        o_ref[:, sl] = out
        carry = out[:, LANES - 1:LANES]


def _cumsum_rows(x):
    rows, length = x.shape
    rb = _pick_tile(rows, 8)
    return pl.pallas_call(
        _cumsum_kernel,
        grid=(rows // rb,),
        in_specs=[pl.BlockSpec((rb, length), lambda i: (i, 0))],
        out_specs=pl.BlockSpec((rb, length), lambda i: (i, 0)),
        out_shape=jax.ShapeDtypeStruct((rows, length), F32),
        compiler_params=_params(("parallel",)),
        name="cumsum",
    )(x)


def _split3(x):
    hi = x.astype(BF16).astype(F32)
    r = x - hi
    mid = r.astype(BF16).astype(F32)
    return hi, mid, (r - mid).astype(BF16).astype(F32)


def _decay_lanes(c, query_side):
    rows = c.shape[0]
    lane = lax.broadcasted_iota(jnp.int32, (rows, LANES), 1)
    hi, mid, lo = _split3(c if query_side else -c)
    if query_side:
        out = jnp.where(lane == 0, hi, jnp.where(lane == 1, mid, jnp.where(lane == 2, lo,
                                                                          jnp.where(lane < 6, 1.0, 0.0))))
    else:
        out = jnp.where(lane < 3, 1.0, jnp.where(lane == 3, hi, jnp.where(lane == 4, mid,
                                                                         jnp.where(lane == 5, lo, 0.0))))
    return out.astype(BF16)


def _attn_prompt_kernel(q_ref, k_ref, v_ref, cum_ref, g_ref, o_ref, qa_scr, ka_scr, va_scr, m_scr, acc_scr,
                        p_scr, alpha_scr, *, scale):
    h = pl.program_id(1)
    qi = pl.program_id(2)
    tq = q_ref.shape[0]
    seq = k_ref.shape[0]
    nh = cum_ref.shape[1]
    c1 = scale * 1.4426950408889634
    head_lane = lax.broadcasted_iota(jnp.int32, (tq, nh), 1)

    def head_col(x):
        return jnp.sum(jnp.where(head_lane == h, x, 0.0), axis=-1, keepdims=True) * (1.0 / scale)

    @pl.when(qi == 0)
    def _():
        for j in range(seq // tq):
            rows = slice(j * tq, (j + 1) * tq)
            ka_scr[rows, :LANES] = k_ref[rows, :].astype(BF16)
            ka_scr[rows, LANES:] = _decay_lanes(head_col(cum_ref[rows, :]), False)
            va_scr[rows, :LANES] = v_ref[rows, :].astype(BF16)
            va_scr[rows, LANES:] = jnp.ones((tq, LANES), BF16)

    q0 = pl.multiple_of(qi * tq, tq)
    qa_scr[:, :LANES] = q_ref[...].astype(BF16)
    qa_scr[:, LANES:] = _decay_lanes(head_col(cum_ref[pl.ds(q0, tq), :]), True)
    m_scr[...] = jnp.full(m_scr.shape, NEG_INF, F32)
    acc_scr[...] = jnp.zeros(acc_scr.shape, F32)
    p_scr[...] = jnp.zeros(p_scr.shape, BF16)
    alpha_scr[...] = jnp.ones(alpha_scr.shape, F32)

    def flush(kprev):
        k0 = pl.multiple_of(kprev * tq, tq)
        acc_scr[...] = alpha_scr[...] * acc_scr[...] + jnp.dot(p_scr[...], va_scr[pl.ds(k0, tq), :],
                                                              preferred_element_type=F32)

    def tile(ki, masked):
        flush(jnp.maximum(ki - 1, 0))
        k0 = pl.multiple_of(ki * tq, tq)
        s = lax.dot_general(qa_scr[...], ka_scr[pl.ds(k0, tq), :], (((1,), (1,)), ((), ())),
                            preferred_element_type=F32)
        if masked:
            qpos = lax.broadcasted_iota(jnp.int32, (tq, tq), 0)
            kpos = lax.broadcasted_iota(jnp.int32, (tq, tq), 1)
            s = jnp.where(kpos <= qpos, s, NEG_INF)
        m_old = m_scr[...]
        m_new = jnp.maximum(m_old, jnp.max(s, axis=-1, keepdims=True))
        p_scr[...] = jnp.exp2(s * c1 - m_new * c1).astype(BF16)
        alpha_scr[...] = jnp.exp2((m_old - m_new) * c1)
        m_scr[...] = m_new

    def body(ki, carry):
        tile(ki, False)
        return carry

    lax.fori_loop(0, qi, body, 0)
    tile(qi, True)
    flush(qi)
    o_ref[...] = _rms(acc_scr[:, :LANES] / acc_scr[:, LANES:], g_ref[0])


def _attn_prompt(q, k, v, cum_col, g_fox, batch, seq, tq):
    n, width = q.shape
    nh = width // LANES
    nt = seq // tq
    scale = float(LANES) ** -0.5
    return pl.pallas_call(
        functools.partial(_attn_prompt_kernel, scale=scale),
        grid=(batch, nh, nt),
        in_specs=[pl.BlockSpec((tq, LANES), lambda b, h, qi: (b * nt + qi, h)),
                  pl.BlockSpec((seq, LANES), lambda b, h, qi: (b, h)),
                  pl.BlockSpec((seq, LANES), lambda b, h, qi: (b, h)),
                  pl.BlockSpec((seq, nh), lambda b, h, qi: (b, 0)),
                  pl.BlockSpec((1, 1, LANES), lambda b, h, qi: (h, 0, 0))],
        out_specs=pl.BlockSpec((tq, LANES), lambda b, h, qi: (b * nt + qi, h)),
        out_shape=jax.ShapeDtypeStruct((n, width), F32),
        scratch_shapes=[pltpu.VMEM((tq, 2 * LANES), BF16), pltpu.VMEM((seq, 2 * LANES), BF16),
                        pltpu.VMEM((seq, 2 * LANES), BF16), pltpu.VMEM((tq, 1), F32),
                        pltpu.VMEM((tq, 2 * LANES), F32), pltpu.VMEM((tq, tq), BF16), pltpu.VMEM((tq, 1), F32)],
        compiler_params=_params(("parallel", "parallel", "arbitrary")),
        name="attn_prompt",
    )(q, k, v, cum_col, g_fox)


def _attn_sample_kernel(q_ref, kn_ref, vn_ref, kc_ref, vc_ref, cq_ref, ckc_ref, ckn_ref, g_ref, o_ref, *, scale):
    ts, nh = cq_ref.shape
    dn = (((1,), (1,)), ((), ()))
    qpos = lax.broadcasted_iota(jnp.int32, (ts, ts), 0)
    kpos = lax.broadcasted_iota(jnp.int32, (ts, ts), 1)
    for h in range(nh):
        sl = slice(h * LANES, (h + 1) * LANES)
        q = q_ref[:, sl].astype(BF16)
        cq = cq_ref[:, h:h + 1]
        s1 = lax.dot_general(q, kc_ref[:, h, :].astype(BF16), dn, preferred_element_type=F32) * scale
        s1 = s1 + (cq - ckc_ref[h])
        s2 = lax.dot_general(q, kn_ref[:, sl].astype(BF16), dn, preferred_element_type=F32) * scale
        s2 = jnp.where(kpos <= qpos, s2 + (cq - ckn_ref[h]), NEG_INF)
        m = jnp.maximum(jnp.max(s1, axis=-1, keepdims=True), jnp.max(s2, axis=-1, keepdims=True))
        p1 = jnp.exp(s1 - m)
        p2 = jnp.exp(s2 - m)
        l = jnp.sum(p1, axis=-1, keepdims=True) + jnp.sum(p2, axis=-1, keepdims=True)
        o = (jnp.dot(p1.astype(BF16), vc_ref[:, h, :].astype(BF16), preferred_element_type=F32)
             + jnp.dot(p2.astype(BF16), vn_ref[:, sl].astype(BF16), preferred_element_type=F32)) / l
        o_ref[:, sl] = _rms(o, g_ref[:, sl])


def _attn_sample(q, k_new, v_new, k_cache, v_cache, cq_col, ck_cache, ck_new, g_fox_row, ts):
    n, width = q.shape
    nb, past, nh, _ = k_cache.shape
    scale = float(LANES) ** -0.5
    tok = pl.BlockSpec((ts, width), lambda b: (b, 0))
    cache = pl.BlockSpec((None, past, nh, LANES), lambda b: (b, 0, 0, 0))
    return pl.pallas_call(
        functools.partial(_attn_sample_kernel, scale=scale),
        grid=(nb,),
        in_specs=[tok, tok, tok, cache, cache,
                  pl.BlockSpec((ts, nh), lambda b: (b, 0)),
                  pl.BlockSpec((nh, 1, past), lambda b: (b, 0, 0)),
                  pl.BlockSpec((nh, 1, ts), lambda b: (b, 0, 0)),
                  pl.BlockSpec((1, width), lambda b: (0, 0))],
        out_specs=tok,
        out_shape=jax.ShapeDtypeStruct((n, width), F32),
        compiler_params=_params(("parallel",)),
        name="attn_sample",
    )(q, k_new, v_new, k_cache, v_cache, cq_col, ck_cache, ck_new, g_fox_row)


def _merge_kernel(x_ref, fox_ref, us_ref, vs_ref, ws_ref, bs_ref, gs_ref, wo1_ref, wo2_ref, o_ref, cat_scr):
    tm = x_ref.shape[0]
    ng, chunk, _ = ws_ref.shape
    r = lax.broadcasted_iota(jnp.int32, (chunk, chunk), 0)
    c = lax.broadcasted_iota(jnp.int32, (chunk, chunk), 1)
    for g in range(ng):
        sl = slice(g * LANES, (g + 1) * LANES)
        wg = jnp.where(c <= r, ws_ref[g], 0.0).astype(BF16)
        bias = bs_ref[g]
        for j in range(tm // chunk):
            rows = slice(j * chunk, (j + 1) * chunk)
            mixed = jnp.dot(wg, vs_ref[rows, sl].astype(BF16), preferred_element_type=F32) + bias
            cat_scr[rows, sl] = _rms(us_ref[rows, sl] * mixed, gs_ref[:, sl]).astype(BF16)
    o_ref[...] = (x_ref[...]
                  + jnp.dot(fox_ref[...].astype(BF16), wo1_ref[...], preferred_element_type=F32)
                  + jnp.dot(cat_scr[...], wo2_ref[...], preferred_element_type=F32))


def _merge(x2d, fox_n, us, vs, w_s, b_s_col, g_sgu_row, wo1, wo2, tm):
    n, d = x2d.shape
    width = fox_n.shape[1]
    ng, chunk, _ = w_s.shape
    full = lambda shape: pl.BlockSpec(shape, lambda i: (0,) * len(shape))
    tok = pl.BlockSpec((tm, width), lambda i: (i, 0))
    row = pl.BlockSpec((tm, d), lambda i: (i, 0))
    return pl.pallas_call(
        _merge_kernel,
        grid=(n // tm,),
        in_specs=[row, tok, tok, tok, full((ng, chunk, chunk)), full((ng, chunk, 1)), full((1, width)),
                  full((width, d)), full((width, d))],
        out_specs=row,
        out_shape=jax.ShapeDtypeStruct((n, d), F32),
        scratch_shapes=[pltpu.VMEM((tm, width), BF16)],
        compiler_params=_params(("parallel",)),
        name="merge",
    )(x2d, fox_n, us, vs, w_s, b_s_col, g_sgu_row, wo1, wo2)


def _top_rows(s, k, ids=None, payload=None):
    if ids is None:
        ids = lax.broadcasted_iota(jnp.int32, s.shape, 0)
    big = jnp.int32(2 ** 30)
    vals, picks = [], []
    for _ in range(k):
        m = jnp.max(s, axis=0, keepdims=True)
        i = jnp.min(jnp.where(s == m, ids, big), axis=0, keepdims=True)
        hit = ids == i
        vals.append(m)
        picks.append(i if payload is None else jnp.max(jnp.where(hit, payload, -1), axis=0, keepdims=True))
        s = jnp.where(hit, -jnp.inf, s)
    return jnp.concatenate(vals, axis=0), jnp.concatenate(picks, axis=0)


def _stair_blocks(kk):
    need = {(a, j) for a in range(kk) for j in range(kk) if (a + 1) * (j + 1) <= kk}
    blocks = []
    for a in range(kk):
        nj = kk // (a + 1)
        if nj >= SUBLANES:
            for j0 in range(0, nj, SUBLANES):
                blocks.append(("row", a, j0, 0, min(SUBLANES, nj - j0)))
                need -= {(a, j0 + r) for r in range(SUBLANES)}
    for j in range(kk):
        for a0 in range(0, kk, SUBLANES):
            rs = [r for r in range(SUBLANES) if (a0 + r, j) in need]
            if rs:
                assert rs == list(range(rs[0], rs[-1] + 1))
                blocks.append(("col", a0, j, rs[0], rs[-1] + 1))
                need -= {(a0 + r, j) for r in rs}
    assert not need
    return blocks


def _pair_candidates(sv, si, n_keys):
    kk, cols = sv[0].shape
    sub = lax.broadcasted_iota(jnp.int32, (SUBLANES, cols), 0)
    vals, flat, expert = [], [], []
    for kind, a0, j0, lo, hi in _stair_blocks(kk):
        if kind == "row":
            v = sv[0][a0:a0 + 1] + sv[1][j0:j0 + SUBLANES]
            e = si[0][a0:a0 + 1] * n_keys + si[1][j0:j0 + SUBLANES]
            f = a0 * kk + j0 + sub
        else:
            v = sv[0][a0:a0 + SUBLANES] + sv[1][j0:j0 + 1]
            e = si[0][a0:a0 + SUBLANES] * n_keys + si[1][j0:j0 + 1]
            f = (a0 + sub) * kk + j0
        if (lo, hi) != (0, SUBLANES):
            v = jnp.where(sub < lo, -jnp.inf, jnp.where(sub < hi, v, -jnp.inf))
        vals.append(v)
        flat.append(f)
        expert.append(e)
    return jnp.concatenate(vals, axis=0), jnp.concatenate(flat, axis=0), jnp.concatenate(expert, axis=0)


def _route_kernel(y_ref, g_ref, wq_ref, sk_ref, h_ref, idx_ref, gate_ref, hb_scr):
    n_keys = sk_ref.shape[2]
    kk = PEER_TOPK

    @pl.when(pl.program_id(1) == 0)
    def _():
        h = _rms(y_ref[...], g_ref[...])
        h_ref[...] = h
        hb_scr[...] = h.astype(BF16)

    q = jnp.dot(hb_scr[...], wq_ref[...], preferred_element_type=F32).astype(BF16)
    dn = (((1,), (1,)), ((), ()))
    sv, si = [], []
    for c in range(2):
        st = lax.dot_general(sk_ref[0, c], q[:, c * LANES:(c + 1) * LANES], dn, preferred_element_type=F32)
        v, i = _top_rows(st, kk)
        sv.append(v)
        si.append(i)
    cand, flat, expert = _pair_candidates(sv, si, n_keys)
    top_s, top_i = _top_rows(cand, kk, ids=flat, payload=expert)
    e = jnp.exp(top_s - top_s[0:1])
    gate_ref[...] = e / jnp.sum(e, axis=0, keepdims=True)
    idx_ref[...] = top_i


def _route(y2d, g_row, w_q, sub_keys, tm):
    n, d = y2d.shape
    n_heads, _, n_keys, half = sub_keys.shape
    assert n_keys == LANES and half == LANES
    sel = n_heads * PEER_TOPK
    return pl.pallas_call(
        _route_kernel,
        grid=(n // tm, n_heads),
        in_specs=[pl.BlockSpec((tm, d), lambda i, hd: (i, 0)),
                  pl.BlockSpec((1, d), lambda i, hd: (0, 0)),
                  pl.BlockSpec((d, 2 * half), lambda i, hd: (0, hd)),
                  pl.BlockSpec((1, 2, n_keys, half), lambda i, hd: (hd, 0, 0, 0))],
        out_specs=[pl.BlockSpec((tm, d), lambda i, hd: (i, 0)),
                   pl.BlockSpec((PEER_TOPK, tm), lambda i, hd: (hd, i)),
                   pl.BlockSpec((PEER_TOPK, tm), lambda i, hd: (hd, i))],
        out_shape=[jax.ShapeDtypeStruct((n, d), F32),
                   jax.ShapeDtypeStruct((sel, n), jnp.int32),
                   jax.ShapeDtypeStruct((sel, n), F32)],
        scratch_shapes=[pltpu.VMEM((tm, d), BF16)],
        compiler_params=_params(("parallel", "arbitrary")),
        name="route",
    )(y2d, g_row, w_q, sub_keys)


def _experts_kernel(idx_ref, h_ref, gate_ref, y_ref, uv_ref, o_ref, *scratch):
    tb, d = h_ref.shape
    sel = gate_ref.shape[0]
    bufs, sem = scratch[:-1], scratch[-1]
    nslot = len(bufs)
    ahead = nslot - 1

    def gather(t, slot):
        for j in range(sel):
            e = idx_ref[t * sel + j]
            pltpu.make_async_copy(uv_ref.at[e], bufs[slot].at[pl.ds(j, 1), :], sem.at[slot]).start(priority=j % 2)

    def wait(slot):
        pltpu.make_async_copy(bufs[slot], bufs[slot], sem.at[slot]).wait()

    for t0 in range(ahead):
        gather(t0, t0)
    lane = lax.broadcasted_iota(jnp.int32, gate_ref.shape, 1)
    half = d // 2

    def unpack(w):
        return (lax.bitcast_convert_type(w << 16, F32), lax.bitcast_convert_type(w & jnp.int32(-65536), F32))

    def group(g, carry):
        base = pl.multiple_of(g * SUBLANES, SUBLANES)
        x8 = h_ref[pl.ds(base, SUBLANES), :]
        rows = []
        for i in range(SUBLANES):
            t = base + i
            slot = i % nslot
            buf = bufs[slot]
            gather(jnp.minimum(t + ahead, tb - 1), (i + ahead) % nslot)
            wait(slot)
            acc = jnp.zeros((sel, LANES), F32)
            for c in range(half // LANES):
                lo, hi = unpack(buf[:, c * LANES:(c + 1) * LANES])
                acc = (acc + lo * x8[i:i + 1, c * LANES:(c + 1) * LANES]
                       + hi * x8[i:i + 1, half + c * LANES:half + (c + 1) * LANES])
            dots = jnp.sum(acc, axis=-1, keepdims=True)
            gate = jnp.sum(jnp.where(lane == t, gate_ref[...], 0.0), axis=-1, keepdims=True)
            act = jnp.broadcast_to(jax.nn.gelu(dots) * gate, (sel, LANES))
            y_lo, y_hi = [], []
            for c in range(half // LANES):
                lo, hi = unpack(buf[:, half + c * LANES:half + (c + 1) * LANES])
                y_lo.append(jnp.sum(lo * act, axis=0, keepdims=True))
                y_hi.append(jnp.sum(hi * act, axis=0, keepdims=True))
            rows.append(jnp.concatenate(y_lo + y_hi, axis=1))
        o_ref[pl.ds(base, SUBLANES), :] = y_ref[pl.ds(base, SUBLANES), :] + jnp.concatenate(rows, axis=0)
        return carry

    lax.fori_loop(0, tb // SUBLANES, group, 0)
    for extra in range(ahead):
        wait((tb + extra) % nslot)


def _experts(idx_flat, h2d, gate_t, y2d, uv, tb):
    n, d = h2d.shape
    sel = gate_t.shape[0]
    assert tb % SUBLANES == 0 and SUBLANES % EXPERT_SLOTS == 0
    return pl.pallas_call(
        _experts_kernel,
        grid=(n // tb,),
        in_specs=[pl.BlockSpec((tb * sel,), lambda i: (i,), memory_space=pltpu.SMEM),
                  pl.BlockSpec((tb, d), lambda i: (i, 0)),
                  pl.BlockSpec((sel, tb), lambda i: (0, i)),
                  pl.BlockSpec((tb, d), lambda i: (i, 0)),
                  pl.BlockSpec(memory_space=pl.ANY)],
        out_specs=pl.BlockSpec((tb, d), lambda i: (i, 0)),
        out_shape=jax.ShapeDtypeStruct((n, d), F32),
        scratch_shapes=[pltpu.VMEM((sel, d), jnp.int32)] * EXPERT_SLOTS + [pltpu.SemaphoreType.DMA((EXPERT_SLOTS,))],
        compiler_params=_params(("arbitrary",)),
        name="experts",
    )(idx_flat, h2d, gate_t, y2d, uv)


def _pack_bf16_halves(x):
    half = x.shape[1] // 2
    bits = lax.bitcast_convert_type(lax.reduce_precision(x, exponent_bits=8, mantissa_bits=7), jnp.uint32)
    return lax.bitcast_convert_type((bits[:, :half] >> 16) | bits[:, half:], jnp.int32)


def _pick_tile(n, want):
    t = min(n, want)
    assert n % t == 0, (n, t)
    return t


def _layer_front(x2d, p):
    n, _ = x2d.shape
    tm = _pick_tile(n, 512)
    return _inproj(x2d, p["norm_mix_g"], p["w_main"], p["w_f_t"], p["b_f"], p["q_g"], p["k_g"], p["vs_g"], tm)


def _layer_back(x2d, fox_n, us, vs, p, chunk):
    n, _ = x2d.shape
    w_s = p["w_s"][:, :chunk, :chunk]
    b_s = p["b_s"][:, :chunk, None]
    y1 = _merge(x2d, fox_n, us, vs, w_s, b_s, p["g_sgu"], p["wo1"], p["wo2"], _pick_tile(n, 256))
    h2, idx_t, gate_t = _route(y1, p["norm_ffn_g"], p["w_q"], p["sub_keys"], _pick_tile(n, 256))
    idx_flat = idx_t.T.reshape(-1)
    return _experts(idx_flat, h2, gate_t, y1, p["uv"], _pick_tile(n, 128))


def kernel(x_prompt, x_sample, cache_fox_k, cache_fox_v, cache_fox_logf, norm_mix_g, w_in, b_f, q_norm_g, k_norm_g, sgu_v_norm_g, w_s, b_s, fox_out_norm_g, sgu_out_norm_g, w_out, norm_ffn_g, peer_w_q, peer_sub_keys, peer_u, peer_v):
    depth = w_in.shape[0]
    batch, seq, d = x_prompt.shape
    dec_batch, dec_seq, _ = x_sample.shape
    past = cache_fox_k.shape[2]
    nh, dh = cache_fox_k.shape[3], cache_fox_k.shape[4]
    ng, dg = sgu_v_norm_g.shape[1], sgu_v_norm_g.shape[2]
    assert dh == LANES and dg == LANES and nh == ng
    width = nh * dh
    assert w_in.shape[2] == 3 * width + nh + 2 * width

    y_p = x_prompt.reshape(batch * seq, d)
    y_s = x_sample.reshape(dec_batch * dec_seq, d)
    outs = [[] for _ in range(7)]
    for l in range(depth):
        wl = w_in[l]
        p = {
            "norm_mix_g": norm_mix_g[l][None, :],
            "w_main": jnp.concatenate([wl[:, :3 * width], wl[:, 3 * width + nh:]], axis=1).astype(BF16),
            "w_f_t": wl[:, 3 * width:3 * width + nh].T.astype(BF16),
            "b_f": b_f[l][:, None],
            "q_g": jnp.tile(q_norm_g[l], nh)[None, :],
            "k_g": jnp.tile(k_norm_g[l], nh)[None, :],
            "vs_g": sgu_v_norm_g[l].reshape(1, width),
            "w_s": w_s[l],
            "b_s": b_s[l],
            "g_sgu": sgu_out_norm_g[l].reshape(1, width),
            "wo1": w_out[l][:width].astype(BF16),
            "wo2": w_out[l][width:].astype(BF16),
            "norm_ffn_g": norm_ffn_g[l][None, :],
            "w_q": peer_w_q[l].astype(BF16),
            "sub_keys": peer_sub_keys[l].astype(BF16),
            "uv": jnp.concatenate([_pack_bf16_halves(peer_u[l]), _pack_bf16_halves(peer_v[l])], axis=1)[:, None, :],
        }
        g_fox = fox_out_norm_g[l]

        q, k, v, us, vs, lf_t = _layer_front(y_s, p)
        lf_new = lf_t.reshape(nh, dec_batch, dec_seq).transpose(1, 0, 2)
        lf_all = jnp.concatenate([cache_fox_logf[l].astype(F32).transpose(0, 2, 1), lf_new], axis=2)
        total = past + dec_seq
        padded = -(-total // LANES) * LANES
        lf_all = jnp.pad(lf_all, ((0, 0), (0, 0), (0, padded - total))).reshape(dec_batch * nh, padded)
        cum = _cumsum_rows(lf_all).reshape(dec_batch, nh, padded)
        ck_cache = cum[:, :, :past].reshape(dec_batch * nh, 1, past)
        ck_new = cum[:, :, past:total]
        cq_col = ck_new.transpose(0, 2, 1).reshape(dec_batch * dec_seq, nh)
        fox_n = _attn_sample(q, k, v, cache_fox_k[l], cache_fox_v[l], cq_col, ck_cache,
                             ck_new.reshape(dec_batch * nh, 1, dec_seq), g_fox.reshape(1, width), dec_seq)
        y_s = _layer_back(y_s, fox_n, us, vs, p, dec_seq)
        outs[3].append(k.reshape(dec_batch, dec_seq, nh, dh))
        outs[4].append(v.reshape(dec_batch, dec_seq, nh, dh))
        outs[5].append(lf_t.T.reshape(dec_batch, dec_seq, nh))
        outs[6].append(vs.reshape(dec_batch, dec_seq, ng, dg))

        q, k, v, us, vs, lf_t = _layer_front(y_p, p)
        lf_rows = lf_t.reshape(nh, batch, seq).transpose(1, 0, 2).reshape(batch * nh, seq)
        cum = _cumsum_rows(lf_rows)
        cq_col = cum.reshape(batch, nh, seq).transpose(0, 2, 1).reshape(batch * seq, nh)
        fox_n = _attn_prompt(q, k, v, cq_col, g_fox[:, None, :], batch, seq, _pick_tile(seq, 512))
        y_p = _layer_back(y_p, fox_n, us, vs, p, SGU_CHUNK)
        outs[0].append(k.reshape(batch, seq, nh, dh))
        outs[1].append(v.reshape(batch, seq, nh, dh))
        outs[2].append(lf_t.T.reshape(batch, seq, nh))

    return (y_p.reshape(batch, seq, d), y_s.reshape(dec_batch, dec_seq, d)) + tuple(jnp.stack(o) for o in outs)
```

```python
import functools

import jax
import jax.numpy as jnp
from jax import lax
from jax.experimental import pallas as pl
from jax.experimental.pallas import tpu as pltpu

EPS = 1e-6
NEG_INF = -1e30
LANES = 128
SUBLANES = 8
MXU_COLS = 256
PEER_TOPK = 16
SGU_CHUNK = 128
EXPERT_SLOTS = 4
VMEM_LIMIT = 52 * 1024 * 1024

F32 = jnp.float32
BF16 = jnp.bfloat16


def _params(sem):
    return pltpu.CompilerParams(dimension_semantics=sem, vmem_limit_bytes=VMEM_LIMIT)


def _rms(x, g):
    return x * lax.rsqrt(jnp.mean(x * x, axis=-1, keepdims=True) + EPS) * g


def _log_sigmoid(x):
    return jnp.minimum(x, 0.0) - jnp.log1p(jnp.exp(-jnp.abs(x)))


def _inproj_kernel(x_ref, g_ref, w_ref, wf_ref, bf_ref, qg_ref, kg_ref, vg_ref,
                   q_ref, k_ref, v_ref, us_ref, vs_ref, lf_ref, h_scr):
    s = pl.program_id(1)
    width = w_ref.shape[1]

    @pl.when(s == 0)
    def _():
        hb = _rms(x_ref[...], g_ref[...]).astype(BF16)
        h_scr[...] = hb
        fl = lax.dot_general(wf_ref[...], hb, (((1,), (1,)), ((), ())), preferred_element_type=F32)
        lf_ref[...] = _log_sigmoid(fl + bf_ref[...])

    def headwise(o_ref, fn):
        for c0 in range(0, width, MXU_COLS):
            z = jnp.dot(h_scr[...], w_ref[:, c0:c0 + MXU_COLS], preferred_element_type=F32)
            for h0 in range(0, MXU_COLS, LANES):
                sl = slice(c0 + h0, c0 + h0 + LANES)
                o_ref[:, sl] = fn(z[:, h0:h0 + LANES], sl)

    @pl.when(s == 0)
    def _():
        headwise(q_ref, lambda zh, sl: _rms(zh, qg_ref[:, sl]))

    @pl.when(s == 1)
    def _():
        headwise(k_ref, lambda zh, sl: _rms(zh, kg_ref[:, sl]))

    @pl.when(s == 2)
    def _():
        headwise(v_ref, lambda zh, sl: zh)

    @pl.when(s == 3)
    def _():
        headwise(us_ref, lambda zh, sl: jax.nn.gelu(zh))

    @pl.when(s == 4)
    def _():
        headwise(vs_ref, lambda zh, sl: _rms(jax.nn.gelu(zh), vg_ref[:, sl]))


def _inproj(x2d, norm_g, w_main, w_f_t, b_f, q_g, k_g, vs_g, tm):
    n, d = x2d.shape
    width = w_main.shape[1] // 5
    nh = w_f_t.shape[0]
    full = lambda shape: pl.BlockSpec(shape, lambda i, s: (0,) * len(shape))
    tok = pl.BlockSpec((tm, width), lambda i, s: (i, 0))
    out_sds = jax.ShapeDtypeStruct((n, width), F32)
    return pl.pallas_call(
        _inproj_kernel,
        grid=(n // tm, 5),
        in_specs=[pl.BlockSpec((tm, d), lambda i, s: (i, 0)), full((1, d)),
                  pl.BlockSpec((d, width), lambda i, s: (0, s)), full((nh, d)), full((nh, 1)),
                  full((1, width)), full((1, width)), full((1, width))],
        out_specs=[tok, tok, tok, tok, tok, pl.BlockSpec((nh, tm), lambda i, s: (0, i))],
        out_shape=[out_sds] * 5 + [jax.ShapeDtypeStruct((nh, n), F32)],
        scratch_shapes=[pltpu.VMEM((tm, d), BF16)],
        compiler_params=_params(("parallel", "arbitrary")),
        name="inproj",
    )(x2d, norm_g, w_main, w_f_t, b_f, q_g, k_g, vs_g)


def _cumsum_kernel(x_ref, o_ref):
    rows, length = x_ref.shape
    r = lax.broadcasted_iota(jnp.int32, (LANES, LANES), 0)
    c = lax.broadcasted_iota(jnp.int32, (LANES, LANES), 1)
    tri = jnp.where(r <= c, 1.0, 0.0).astype(BF16)
    carry = jnp.zeros((rows, 1), F32)
    for j in range(length // LANES):
        sl = slice(j * LANES, (j + 1) * LANES)
        x = x_ref[:, sl]
        hi = x.astype(BF16)
        r1 = x - hi.astype(F32)
        mid = r1.astype(BF16)
        lo = (r1 - mid.astype(F32)).astype(BF16)
        cs = (jnp.dot(hi, tri, preferred_element_type=F32)
              + jnp.dot(mid, tri, preferred_element_type=F32)
              + jnp.dot(lo, tri, preferred_element_type=F32))
        out = cs + carry
        o_ref[:, sl] = out
        carry = out[:, LANES - 1:LANES]


def _cumsum_rows(x):
    rows, length = x.shape
    rb = _pick_tile(rows, 8)
    return pl.pallas_call(
        _cumsum_kernel,
        grid=(rows // rb,),
        in_specs=[pl.BlockSpec((rb, length), lambda i: (i, 0))],
        out_specs=pl.BlockSpec((rb, length), lambda i: (i, 0)),
        out_shape=jax.ShapeDtypeStruct((rows, length), F32),
        compiler_params=_params(("parallel",)),
        name="cumsum",
    )(x)


def _split3(x):
    hi = x.astype(BF16).astype(F32)
    r = x - hi
    mid = r.astype(BF16).astype(F32)
    return hi, mid, (r - mid).astype(BF16).astype(F32)


def _decay_lanes(c, query_side):
    rows = c.shape[0]
    lane = lax.broadcasted_iota(jnp.int32, (rows, LANES), 1)
    hi, mid, lo = _split3(c if query_side else -c)
    if query_side:
        out = jnp.where(lane == 0, hi, jnp.where(lane == 1, mid, jnp.where(lane == 2, lo,
                                                                          jnp.where(lane < 6, 1.0, 0.0))))
    else:
        out = jnp.where(lane < 3, 1.0, jnp.where(lane == 3, hi, jnp.where(lane == 4, mid,
                                                                         jnp.where(lane == 5, lo, 0.0))))
    return out.astype(BF16)


def _attn_prompt_kernel(q_ref, k_ref, v_ref, cum_ref, g_ref, o_ref, qa_scr, ka_scr, va_scr, m_scr, acc_scr,
                        p_scr, alpha_scr, *, scale):
    h = pl.program_id(1)
    qi = pl.program_id(2)
    tq = q_ref.shape[0]
    seq = k_ref.shape[0]
    nh = cum_ref.shape[1]
    c1 = scale * 1.4426950408889634
    head_lane = lax.broadcasted_iota(jnp.int32, (tq, nh), 1)

    def head_col(x):
        return jnp.sum(jnp.where(head_lane == h, x, 0.0), axis=-1, keepdims=True) * (1.0 / scale)

    @pl.when(qi == 0)
    def _():
        for j in range(seq // tq):
            rows = slice(j * tq, (j + 1) * tq)
            ka_scr[rows, :LANES] = k_ref[rows, :].astype(BF16)
            ka_scr[rows, LANES:] = _decay_lanes(head_col(cum_ref[rows, :]), False)
            va_scr[rows, :LANES] = v_ref[rows, :].astype(BF16)
            va_scr[rows, LANES:] = jnp.ones((tq, LANES), BF16)

    q0 = pl.multiple_of(qi * tq, tq)
    qa_scr[:, :LANES] = q_ref[...].astype(BF16)
    qa_scr[:, LANES:] = _decay_lanes(head_col(cum_ref[pl.ds(q0, tq), :]), True)
    m_scr[...] = jnp.full(m_scr.shape, NEG_INF, F32)
    acc_scr[...] = jnp.zeros(acc_scr.shape, F32)
    p_scr[...] = jnp.zeros(p_scr.shape, BF16)
    alpha_scr[...] = jnp.ones(alpha_scr.shape, F32)

    def flush(kprev):
        k0 = pl.multiple_of(kprev * tq, tq)
        acc_scr[...] = alpha_scr[...] * acc_scr[...] + jnp.dot(p_scr[...], va_scr[pl.ds(k0, tq), :],
                                                              preferred_element_type=F32)

    def tile(ki, masked):
        flush(jnp.maximum(ki - 1, 0))
        k0 = pl.multiple_of(ki * tq, tq)
        s = lax.dot_general(qa_scr[...], ka_scr[pl.ds(k0, tq), :], (((1,), (1,)), ((), ())),
                            preferred_element_type=F32)
        if masked:
            qpos = lax.broadcasted_iota(jnp.int32, (tq, tq), 0)
            kpos = lax.broadcasted_iota(jnp.int32, (tq, tq), 1)
            s = jnp.where(kpos <= qpos, s, NEG_INF)
        m_old = m_scr[...]
        m_new = jnp.maximum(m_old, jnp.max(s, axis=-1, keepdims=True))
        p_scr[...] = jnp.exp2(s * c1 - m_new * c1).astype(BF16)
        alpha_scr[...] = jnp.exp2((m_old - m_new) * c1)
        m_scr[...] = m_new

    def body(ki, carry):
        tile(ki, False)
        return carry

    lax.fori_loop(0, qi, body, 0)
    tile(qi, True)
    flush(qi)
    o_ref[...] = _rms(acc_scr[:, :LANES] / acc_scr[:, LANES:], g_ref[0])


def _attn_prompt(q, k, v, cum_col, g_fox, batch, seq, tq):
    n, width = q.shape
    nh = width // LANES
    nt = seq // tq
    scale = float(LANES) ** -0.5
    return pl.pallas_call(
        functools.partial(_attn_prompt_kernel, scale=scale),
        grid=(batch, nh, nt),
        in_specs=[pl.BlockSpec((tq, LANES), lambda b, h, qi: (b * nt + qi, h)),
                  pl.BlockSpec((seq, LANES), lambda b, h, qi: (b, h)),
                  pl.BlockSpec((seq, LANES), lambda b, h, qi: (b, h)),
                  pl.BlockSpec((seq, nh), lambda b, h, qi: (b, 0)),
                  pl.BlockSpec((1, 1, LANES), lambda b, h, qi: (h, 0, 0))],
        out_specs=pl.BlockSpec((tq, LANES), lambda b, h, qi: (b * nt + qi, h)),
        out_shape=jax.ShapeDtypeStruct((n, width), F32),
        scratch_shapes=[pltpu.VMEM((tq, 2 * LANES), BF16), pltpu.VMEM((seq, 2 * LANES), BF16),
                        pltpu.VMEM((seq, 2 * LANES), BF16), pltpu.VMEM((tq, 1), F32),
                        pltpu.VMEM((tq, 2 * LANES), F32), pltpu.VMEM((tq, tq), BF16), pltpu.VMEM((tq, 1), F32)],
        compiler_params=_params(("parallel", "parallel", "arbitrary")),
        name="attn_prompt",
    )(q, k, v, cum_col, g_fox)


def _attn_sample_kernel(q_ref, kn_ref, vn_ref, kc_ref, vc_ref, cq_ref, ckc_ref, ckn_ref, g_ref, o_ref, *, scale):
    ts, nh = cq_ref.shape
    dn = (((1,), (1,)), ((), ()))
    qpos = lax.broadcasted_iota(jnp.int32, (ts, ts), 0)
    kpos = lax.broadcasted_iota(jnp.int32, (ts, ts), 1)
    for h in range(nh):
        sl = slice(h * LANES, (h + 1) * LANES)
        q = q_ref[:, sl].astype(BF16)
        cq = cq_ref[:, h:h + 1]
        s1 = lax.dot_general(q, kc_ref[:, h, :].astype(BF16), dn, preferred_element_type=F32) * scale
        s1 = s1 + (cq - ckc_ref[h])
        s2 = lax.dot_general(q, kn_ref[:, sl].astype(BF16), dn, preferred_element_type=F32) * scale
        s2 = jnp.where(kpos <= qpos, s2 + (cq - ckn_ref[h]), NEG_INF)
        m = jnp.maximum(jnp.max(s1, axis=-1, keepdims=True), jnp.max(s2, axis=-1, keepdims=True))
        p1 = jnp.exp(s1 - m)
        p2 = jnp.exp(s2 - m)
        l = jnp.sum(p1, axis=-1, keepdims=True) + jnp.sum(p2, axis=-1, keepdims=True)
        o = (jnp.dot(p1.astype(BF16), vc_ref[:, h, :].astype(BF16), preferred_element_type=F32)
             + jnp.dot(p2.astype(BF16), vn_ref[:, sl].astype(BF16), preferred_element_type=F32)) / l
        o_ref[:, sl] = _rms(o, g_ref[:, sl])


def _attn_sample(q, k_new, v_new, k_cache, v_cache, cq_col, ck_cache, ck_new, g_fox_row, ts):
    n, width = q.shape
    nb, past, nh, _ = k_cache.shape
    scale = float(LANES) ** -0.5
    tok = pl.BlockSpec((ts, width), lambda b: (b, 0))
    cache = pl.BlockSpec((None, past, nh, LANES), lambda b: (b, 0, 0, 0))
    return pl.pallas_call(
        functools.partial(_attn_sample_kernel, scale=scale),
        grid=(nb,),
        in_specs=[tok, tok, tok, cache, cache,
                  pl.BlockSpec((ts, nh), lambda b: (b, 0)),
                  pl.BlockSpec((nh, 1, past), lambda b: (b, 0, 0)),
                  pl.BlockSpec((nh, 1, ts), lambda b: (b, 0, 0)),
                  pl.BlockSpec((1, width), lambda b: (0, 0))],
        out_specs=tok,
        out_shape=jax.ShapeDtypeStruct((n, width), F32),
        compiler_params=_params(("parallel",)),
        name="attn_sample",
    )(q, k_new, v_new, k_cache, v_cache, cq_col, ck_cache, ck_new, g_fox_row)


def _merge_kernel(x_ref, fox_ref, us_ref, vs_ref, ws_ref, bs_ref, gs_ref, wo1_ref, wo2_ref, o_ref, cat_scr):
    tm = x_ref.shape[0]
    ng, chunk, _ = ws_ref.shape
    r = lax.broadcasted_iota(jnp.int32, (chunk, chunk), 0)
    c = lax.broadcasted_iota(jnp.int32, (chunk, chunk), 1)
    for g in range(ng):
        sl = slice(g * LANES, (g + 1) * LANES)
        wg = jnp.where(c <= r, ws_ref[g], 0.0).astype(BF16)
        bias = bs_ref[g]
        for j in range(tm // chunk):
            rows = slice(j * chunk, (j + 1) * chunk)
            mixed = jnp.dot(wg, vs_ref[rows, sl].astype(BF16), preferred_element_type=F32) + bias
            cat_scr[rows, sl] = _rms(us_ref[rows, sl] * mixed, gs_ref[:, sl]).astype(BF16)
    o_ref[...] = (x_ref[...]
                  + jnp.dot(fox_ref[...].astype(BF16), wo1_ref[...], preferred_element_type=F32)
                  + jnp.dot(cat_scr[...], wo2_ref[...], preferred_element_type=F32))


def _merge(x2d, fox_n, us, vs, w_s, b_s_col, g_sgu_row, wo1, wo2, tm):
    n, d = x2d.shape
    width = fox_n.shape[1]
    ng, chunk, _ = w_s.shape
    full = lambda shape: pl.BlockSpec(shape, lambda i: (0,) * len(shape))
    tok = pl.BlockSpec((tm, width), lambda i: (i, 0))
    row = pl.BlockSpec((tm, d), lambda i: (i, 0))
    return pl.pallas_call(
        _merge_kernel,
        grid=(n // tm,),
        in_specs=[row, tok, tok, tok, full((ng, chunk, chunk)), full((ng, chunk, 1)), full((1, width)),
                  full((width, d)), full((width, d))],
        out_specs=row,
        out_shape=jax.ShapeDtypeStruct((n, d), F32),
        scratch_shapes=[pltpu.VMEM((tm, width), BF16)],
        compiler_params=_params(("parallel",)),
        name="merge",
    )(x2d, fox_n, us, vs, w_s, b_s_col, g_sgu_row, wo1, wo2)


def _top_rows(s, k, ids=None, payload=None):
    if ids is None:
        ids = lax.broadcasted_iota(jnp.int32, s.shape, 0).astype(F32)
    vals, picks = [], []
    for _ in range(k):
        m = jnp.max(s, axis=0, keepdims=True)
        i = jnp.min(jnp.where(s == m, ids, jnp.inf), axis=0, keepdims=True)
        hit = ids == i
        vals.append(m)
        picks.append(i if payload is None else jnp.max(jnp.where(hit, payload, -1.0), axis=0, keepdims=True))
        s = jnp.where(hit, -jnp.inf, s)
    return jnp.concatenate(vals, axis=0), jnp.concatenate(picks, axis=0)


def _stair_blocks(kk):
    need = {(a, j) for a in range(kk) for j in range(kk) if (a + 1) * (j + 1) <= kk}
    blocks = []
    for a in range(kk):
        nj = kk // (a + 1)
        if nj >= SUBLANES:
            for j0 in range(0, nj, SUBLANES):
                blocks.append(("row", a, j0, 0, min(SUBLANES, nj - j0)))
                need -= {(a, j0 + r) for r in range(SUBLANES)}
    for j in range(kk):
        for a0 in range(0, kk, SUBLANES):
            rs = [r for r in range(SUBLANES) if (a0 + r, j) in need]
            if rs:
                assert rs == list(range(rs[0], rs[-1] + 1))
                blocks.append(("col", a0, j, rs[0], rs[-1] + 1))
                need -= {(a0 + r, j) for r in rs}
    assert not need
    return blocks


def _pair_candidates(sv, si, n_keys):
    kk, cols = sv[0].shape
    sub = lax.broadcasted_iota(jnp.int32, (SUBLANES, cols), 0)
    vals, flat, expert = [], [], []
    for kind, a0, j0, lo, hi in _stair_blocks(kk):
        if kind == "row":
            v = sv[0][a0:a0 + 1] + sv[1][j0:j0 + SUBLANES]
            e = si[0][a0:a0 + 1] * n_keys + si[1][j0:j0 + SUBLANES]
            f = (a0 * kk + j0 + sub).astype(F32)
        else:
            v = sv[0][a0:a0 + SUBLANES] + sv[1][j0:j0 + 1]
            e = si[0][a0:a0 + SUBLANES] * n_keys + si[1][j0:j0 + 1]
            f = ((a0 + sub) * kk + j0).astype(F32)
        if (lo, hi) != (0, SUBLANES):
            v = jnp.where(sub < lo, -jnp.inf, jnp.where(sub < hi, v, -jnp.inf))
        vals.append(v)
        flat.append(f)
        expert.append(e)
    return jnp.concatenate(vals, axis=0), jnp.concatenate(flat, axis=0), jnp.concatenate(expert, axis=0)


def _route_kernel(y_ref, g_ref, wq_ref, sk_ref, h_ref, idx_ref, gate_ref, hb_scr):
    n_keys = sk_ref.shape[2]
    kk = PEER_TOPK

    @pl.when(pl.program_id(1) == 0)
    def _():
        h = _rms(y_ref[...], g_ref[...])
        h_ref[...] = h
        hb_scr[...] = h.astype(BF16)

    q = jnp.dot(hb_scr[...], wq_ref[...], preferred_element_type=F32).astype(BF16)
    dn = (((1,), (1,)), ((), ()))
    sv, si = [], []
    for c in range(2):
        st = lax.dot_general(sk_ref[0, c], q[:, c * LANES:(c + 1) * LANES], dn, preferred_element_type=F32)
        v, i = _top_rows(st, kk)
        sv.append(v)
        si.append(i)
    cand, flat, expert = _pair_candidates(sv, si, n_keys)
    top_s, top_i = _top_rows(cand, kk, ids=flat, payload=expert)
    e = jnp.exp(top_s - top_s[0:1])
    gate_ref[...] = e / jnp.sum(e, axis=0, keepdims=True)
    idx_ref[...] = top_i.astype(jnp.int32)


def _route(y2d, g_row, w_q, sub_keys, tm):
    n, d = y2d.shape
    n_heads, _, n_keys, half = sub_keys.shape
    assert n_keys == LANES and half == LANES
    sel = n_heads * PEER_TOPK
    return pl.pallas_call(
        _route_kernel,
        grid=(n // tm, n_heads),
        in_specs=[pl.BlockSpec((tm, d), lambda i, hd: (i, 0)),
                  pl.BlockSpec((1, d), lambda i, hd: (0, 0)),
                  pl.BlockSpec((d, 2 * half), lambda i, hd: (0, hd)),
                  pl.BlockSpec((1, 2, n_keys, half), lambda i, hd: (hd, 0, 0, 0))],
        out_specs=[pl.BlockSpec((tm, d), lambda i, hd: (i, 0)),
                   pl.BlockSpec((PEER_TOPK, tm), lambda i, hd: (hd, i)),
                   pl.BlockSpec((PEER_TOPK, tm), lambda i, hd: (hd, i))],
        out_shape=[jax.ShapeDtypeStruct((n, d), F32),
                   jax.ShapeDtypeStruct((sel, n), jnp.int32),
                   jax.ShapeDtypeStruct((sel, n), F32)],
        scratch_shapes=[pltpu.VMEM((tm, d), BF16)],
        compiler_params=_params(("parallel", "arbitrary")),
        name="route",
    )(y2d, g_row, w_q, sub_keys)


def _experts_kernel(idx_ref, h_ref, gate_ref, y_ref, uv_ref, o_ref, *scratch):
    tb, d = h_ref.shape
    sel = gate_ref.shape[0]
    bufs, sem = scratch[:-1], scratch[-1]
    nslot = len(bufs)
    ahead = nslot - 1

    def gather(t, slot):
        for j in range(sel):
            e = idx_ref[t * sel + j]
            pltpu.make_async_copy(uv_ref.at[e], bufs[slot].at[pl.ds(j, 1), :], sem.at[slot]).start(priority=j % 2)

    def wait(slot):
        pltpu.make_async_copy(bufs[slot], bufs[slot], sem.at[slot]).wait()

    for t0 in range(ahead):
        gather(t0, t0)
    lane = lax.broadcasted_iota(jnp.int32, gate_ref.shape, 1)
    half = d // 2

    def unpack(w):
        return (lax.bitcast_convert_type(w << 16, F32), lax.bitcast_convert_type(w & jnp.int32(-65536), F32))

    def group(g, carry):
        base = pl.multiple_of(g * SUBLANES, SUBLANES)
        x8 = h_ref[pl.ds(base, SUBLANES), :]
        rows = []
        for i in range(SUBLANES):
            t = base + i
            slot = i % nslot
            buf = bufs[slot]
            gather(jnp.minimum(t + ahead, tb - 1), (i + ahead) % nslot)
            wait(slot)
            acc = jnp.zeros((sel, LANES), F32)
            for c in range(half // LANES):
                lo, hi = unpack(buf[:, c * LANES:(c + 1) * LANES])
                acc = (acc + lo * x8[i:i + 1, c * LANES:(c + 1) * LANES]
                       + hi * x8[i:i + 1, half + c * LANES:half + (c + 1) * LANES])
            dots = jnp.sum(acc, axis=-1, keepdims=True)
            gate = jnp.sum(jnp.where(lane == t, gate_ref[...], 0.0), axis=-1, keepdims=True)
            act = jnp.broadcast_to(jax.nn.gelu(dots) * gate, (sel, LANES))
            y_lo, y_hi = [], []
            for c in range(half // LANES):
                lo, hi = unpack(buf[:, half + c * LANES:half + (c + 1) * LANES])
                y_lo.append(jnp.sum(lo * act, axis=0, keepdims=True))
                y_hi.append(jnp.sum(hi * act, axis=0, keepdims=True))
            rows.append(jnp.concatenate(y_lo + y_hi, axis=1))
        o_ref[pl.ds(base, SUBLANES), :] = y_ref[pl.ds(base, SUBLANES), :] + jnp.concatenate(rows, axis=0)
        return carry

    lax.fori_loop(0, tb // SUBLANES, group, 0)
    for extra in range(ahead):
        wait((tb + extra) % nslot)


def _experts(idx_flat, h2d, gate_t, y2d, uv, tb):
    n, d = h2d.shape
    sel = gate_t.shape[0]
    assert tb % SUBLANES == 0 and SUBLANES % EXPERT_SLOTS == 0
    return pl.pallas_call(
        _experts_kernel,
        grid=(n // tb,),
        in_specs=[pl.BlockSpec((tb * sel,), lambda i: (i,), memory_space=pltpu.SMEM),
                  pl.BlockSpec((tb, d), lambda i: (i, 0)),
                  pl.BlockSpec((sel, tb), lambda i: (0, i)),
                  pl.BlockSpec((tb, d), lambda i: (i, 0)),
                  pl.BlockSpec(memory_space=pl.ANY)],
        out_specs=pl.BlockSpec((tb, d), lambda i: (i, 0)),
        out_shape=jax.ShapeDtypeStruct((n, d), F32),
        scratch_shapes=[pltpu.VMEM((sel, d), jnp.int32)] * EXPERT_SLOTS + [pltpu.SemaphoreType.DMA((EXPERT_SLOTS,))],
        compiler_params=_params(("arbitrary",)),
        name="experts",
    )(idx_flat, h2d, gate_t, y2d, uv)


def _pack_bf16_halves(x):
    half = x.shape[1] // 2
    bits = lax.bitcast_convert_type(lax.reduce_precision(x, exponent_bits=8, mantissa_bits=7), jnp.uint32)
    return lax.bitcast_convert_type((bits[:, :half] >> 16) | bits[:, half:], jnp.int32)


def _pick_tile(n, want):
    t = min(n, want)
    assert n % t == 0, (n, t)
    return t


def _layer_front(x2d, p):
    n, _ = x2d.shape
    tm = _pick_tile(n, 512)
    return _inproj(x2d, p["norm_mix_g"], p["w_main"], p["w_f_t"], p["b_f"], p["q_g"], p["k_g"], p["vs_g"], tm)


def _layer_back(x2d, fox_n, us, vs, p, chunk):
    n, _ = x2d.shape
    w_s = p["w_s"][:, :chunk, :chunk]
    b_s = p["b_s"][:, :chunk, None]
    y1 = _merge(x2d, fox_n, us, vs, w_s, b_s, p["g_sgu"], p["wo1"], p["wo2"], _pick_tile(n, 256))
    h2, idx_t, gate_t = _route(y1, p["norm_ffn_g"], p["w_q"], p["sub_keys"], _pick_tile(n, 256))
    idx_flat = idx_t.T.reshape(-1)
    return _experts(idx_flat, h2, gate_t, y1, p["uv"], _pick_tile(n, 128))


def kernel(x_prompt, x_sample, cache_fox_k, cache_fox_v, cache_fox_logf, norm_mix_g, w_in, b_f, q_norm_g, k_norm_g, sgu_v_norm_g, w_s, b_s, fox_out_norm_g, sgu_out_norm_g, w_out, norm_ffn_g, peer_w_q, peer_sub_keys, peer_u, peer_v):
    depth = w_in.shape[0]
    batch, seq, d = x_prompt.shape
    dec_batch, dec_seq, _ = x_sample.shape
    past = cache_fox_k.shape[2]
    nh, dh = cache_fox_k.shape[3], cache_fox_k.shape[4]
    ng, dg = sgu_v_norm_g.shape[1], sgu_v_norm_g.shape[2]
    assert dh == LANES and dg == LANES and nh == ng
    width = nh * dh
    assert w_in.shape[2] == 3 * width + nh + 2 * width

    y_p = x_prompt.reshape(batch * seq, d)
    y_s = x_sample.reshape(dec_batch * dec_seq, d)
    outs = [[] for _ in range(7)]
    for l in range(depth):
        wl = w_in[l]
        p = {
            "norm_mix_g": norm_mix_g[l][None, :],
            "w_main": jnp.concatenate([wl[:, :3 * width], wl[:, 3 * width + nh:]], axis=1).astype(BF16),
            "w_f_t": wl[:, 3 * width:3 * width + nh].T.astype(BF16),
            "b_f": b_f[l][:, None],
            "q_g": jnp.tile(q_norm_g[l], nh)[None, :],
            "k_g": jnp.tile(k_norm_g[l], nh)[None, :],
            "vs_g": sgu_v_norm_g[l].reshape(1, width),
            "w_s": w_s[l],
            "b_s": b_s[l],
            "g_sgu": sgu_out_norm_g[l].reshape(1, width),
            "wo1": w_out[l][:width].astype(BF16),
            "wo2": w_out[l][width:].astype(BF16),
            "norm_ffn_g": norm_ffn_g[l][None, :],
            "w_q": peer_w_q[l].astype(BF16),
            "sub_keys": peer_sub_keys[l].astype(BF16),
            "uv": jnp.concatenate([_pack_bf16_halves(peer_u[l]), _pack_bf16_halves(peer_v[l])], axis=1)[:, None, :],
        }
        g_fox = fox_out_norm_g[l]

        q, k, v, us, vs, lf_t = _layer_front(y_s, p)
        lf_new = lf_t.reshape(nh, dec_batch, dec_seq).transpose(1, 0, 2)
        lf_all = jnp.concatenate([cache_fox_logf[l].astype(F32).transpose(0, 2, 1), lf_new], axis=2)
        total = past + dec_seq
        padded = -(-total // LANES) * LANES
        lf_all = jnp.pad(lf_all, ((0, 0), (0, 0), (0, padded - total))).reshape(dec_batch * nh, padded)
        cum = _cumsum_rows(lf_all).reshape(dec_batch, nh, padded)
        ck_cache = cum[:, :, :past].reshape(dec_batch * nh, 1, past)
        ck_new = cum[:, :, past:total]
        cq_col = ck_new.transpose(0, 2, 1).reshape(dec_batch * dec_seq, nh)
        fox_n = _attn_sample(q, k, v, cache_fox_k[l], cache_fox_v[l], cq_col, ck_cache,
                             ck_new.reshape(dec_batch * nh, 1, dec_seq), g_fox.reshape(1, width), dec_seq)
        y_s = _layer_back(y_s, fox_n, us, vs, p, dec_seq)
        outs[3].append(k.reshape(dec_batch, dec_seq, nh, dh))
        outs[4].append(v.reshape(dec_batch, dec_seq, nh, dh))
        outs[5].append(lf_t.T.reshape(dec_batch, dec_seq, nh))
        outs[6].append(vs.reshape(dec_batch, dec_seq, ng, dg))

        q, k, v, us, vs, lf_t = _layer_front(y_p, p)
        lf_rows = lf_t.reshape(nh, batch, seq).transpose(1, 0, 2).reshape(batch * nh, seq)
        cum = _cumsum_rows(lf_rows)
        cq_col = cum.reshape(batch, nh, seq).transpose(0, 2, 1).reshape(batch * seq, nh)
        fox_n = _attn_prompt(q, k, v, cq_col, g_fox[:, None, :], batch, seq, _pick_tile(seq, 512))
        y_p = _layer_back(y_p, fox_n, us, vs, p, SGU_CHUNK)
        outs[0].append(k.reshape(batch, seq, nh, dh))
        outs[1].append(v.reshape(batch, seq, nh, dh))
        outs[2].append(lf_t.T.reshape(batch, seq, nh))

    return (y_p.reshape(batch, seq, d), y_s.reshape(dec_batch, dec_seq, d)) + tuple(jnp.stack(o) for o in outs)
```

```python
import functools

import jax
import jax.numpy as jnp
from jax import lax
from jax.experimental import pallas as pl
from jax.experimental.pallas import tpu as pltpu

EPS = 1e-6
NEG_INF = -1e30
LANES = 128
SUBLANES = 8
MXU_COLS = 256
PEER_TOPK = 16
SGU_CHUNK = 128
EXPERT_SLOTS = 4
VMEM_LIMIT = 52 * 1024 * 1024

F32 = jnp.float32
BF16 = jnp.bfloat16


def _params(sem):
    return pltpu.CompilerParams(dimension_semantics=sem, vmem_limit_bytes=VMEM_LIMIT)


def _rms(x, g):
    return x * lax.rsqrt(jnp.mean(x * x, axis=-1, keepdims=True) + EPS) * g


def _log_sigmoid(x):
    return jnp.minimum(x, 0.0) - jnp.log1p(jnp.exp(-jnp.abs(x)))


def _inproj_kernel(x_ref, g_ref, w_ref, wf_ref, bf_ref, qg_ref, kg_ref, vg_ref,
                   q_ref, k_ref, v_ref, us_ref, vs_ref, lf_ref, h_scr):
    s = pl.program_id(1)
    width = w_ref.shape[1]

    @pl.when(s == 0)
    def _():
        hb = _rms(x_ref[...], g_ref[...]).astype(BF16)
        h_scr[...] = hb
        fl = lax.dot_general(wf_ref[...], hb, (((1,), (1,)), ((), ())), preferred_element_type=F32)
        lf_ref[...] = _log_sigmoid(fl + bf_ref[...])

    def headwise(o_ref, fn):
        for c0 in range(0, width, MXU_COLS):
            z = jnp.dot(h_scr[...], w_ref[:, c0:c0 + MXU_COLS], preferred_element_type=F32)
            for h0 in range(0, MXU_COLS, LANES):
                sl = slice(c0 + h0, c0 + h0 + LANES)
                o_ref[:, sl] = fn(z[:, h0:h0 + LANES], sl)

    @pl.when(s == 0)
    def _():
        headwise(q_ref, lambda zh, sl: _rms(zh, qg_ref[:, sl]))

    @pl.when(s == 1)
    def _():
        headwise(k_ref, lambda zh, sl: _rms(zh, kg_ref[:, sl]))

    @pl.when(s == 2)
    def _():
        headwise(v_ref, lambda zh, sl: zh)

    @pl.when(s == 3)
    def _():
        headwise(us_ref, lambda zh, sl: jax.nn.gelu(zh))

    @pl.when(s == 4)
    def _():
        headwise(vs_ref, lambda zh, sl: _rms(jax.nn.gelu(zh), vg_ref[:, sl]))


def _inproj(x2d, norm_g, w_main, w_f_t, b_f, q_g, k_g, vs_g, tm):
    n, d = x2d.shape
    width = w_main.shape[1] // 5
    nh = w_f_t.shape[0]
    full = lambda shape: pl.BlockSpec(shape, lambda i, s: (0,) * len(shape))
    tok = pl.BlockSpec((tm, width), lambda i, s: (i, 0))
    out_sds = jax.ShapeDtypeStruct((n, width), F32)
    return pl.pallas_call(
        _inproj_kernel,
        grid=(n // tm, 5),
        in_specs=[pl.BlockSpec((tm, d), lambda i, s: (i, 0)), full((1, d)),
                  pl.BlockSpec((d, width), lambda i, s: (0, s)), full((nh, d)), full((nh, 1)),
                  full((1, width)), full((1, width)), full((1, width))],
        out_specs=[tok, tok, tok, tok, tok, pl.BlockSpec((nh, tm), lambda i, s: (0, i))],
        out_shape=[out_sds] * 5 + [jax.ShapeDtypeStruct((nh, n), F32)],
        scratch_shapes=[pltpu.VMEM((tm, d), BF16)],
        compiler_params=_params(("parallel", "arbitrary")),
        name="inproj",
    )(x2d, norm_g, w_main, w_f_t, b_f, q_g, k_g, vs_g)


def _cumsum_kernel(x_ref, o_ref):
    rows, length = x_ref.shape
    r = lax.broadcasted_iota(jnp.int32, (LANES, LANES), 0)
    c = lax.broadcasted_iota(jnp.int32, (LANES, LANES), 1)
    tri = jnp.where(r <= c, 1.0, 0.0).astype(BF16)
    carry = jnp.zeros((rows, 1), F32)
    for j in range(length // LANES):
        sl = slice(j * LANES, (j + 1) * LANES)
        x = x_ref[:, sl]
        hi = x.astype(BF16)
        r1 = x - hi.astype(F32)
        mid = r1.astype(BF16)
        lo = (r1 - mid.astype(F32)).astype(BF16)
        cs = (jnp.dot(hi, tri, preferred_element_type=F32)
              + jnp.dot(mid, tri, preferred_element_type=F32)
              + jnp.dot(lo, tri, preferred_element_type=F32))
        out = cs + carry
        o_ref[:, sl] = out
        carry = out[:, LANES - 1:LANES]


def _cumsum_rows(x):
    rows, length = x.shape
    rb = _pick_tile(rows, 8)
    return pl.pallas_call(
        _cumsum_kernel,
        grid=(rows // rb,),
        in_specs=[pl.BlockSpec((rb, length), lambda i: (i, 0))],
        out_specs=pl.BlockSpec((rb, length), lambda i: (i, 0)),
        out_shape=jax.ShapeDtypeStruct((rows, length), F32),
        compiler_params=_params(("parallel",)),
        name="cumsum",
    )(x)


def _split3(x):
    hi = x.astype(BF16).astype(F32)
    r = x - hi
    mid = r.astype(BF16).astype(F32)
    return hi, mid, (r - mid).astype(BF16).astype(F32)


def _decay_lanes(c, query_side):
    rows = c.shape[0]
    lane = lax.broadcasted_iota(jnp.int32, (rows, LANES), 1)
    hi, mid, lo = _split3(c if query_side else -c)
    if query_side:
        out = jnp.where(lane == 0, hi, jnp.where(lane == 1, mid, jnp.where(lane == 2, lo,
                                                                          jnp.where(lane < 6, 1.0, 0.0))))
    else:
        out = jnp.where(lane < 3, 1.0, jnp.where(lane == 3, hi, jnp.where(lane == 4, mid,
                                                                         jnp.where(lane == 5, lo, 0.0))))
    return out.astype(BF16)


def _attn_prompt_kernel(q_ref, k_ref, v_ref, cum_ref, g_ref, o_ref, qa_scr, ka_scr, va_scr, m_scr, acc_scr,
                        p_scr, alpha_scr, *, scale):
    h = pl.program_id(1)
    qi = pl.program_id(2)
    tq = q_ref.shape[0]
    seq = k_ref.shape[0]
    nh = cum_ref.shape[1]
    c1 = scale * 1.4426950408889634
    head_lane = lax.broadcasted_iota(jnp.int32, (tq, nh), 1)

    def head_col(x):
        return jnp.sum(jnp.where(head_lane == h, x, 0.0), axis=-1, keepdims=True) * (1.0 / scale)

    @pl.when(qi == 0)
    def _():
        for j in range(seq // tq):
            rows = slice(j * tq, (j + 1) * tq)
            ka_scr[rows, :LANES] = k_ref[rows, :].astype(BF16)
            ka_scr[rows, LANES:] = _decay_lanes(head_col(cum_ref[rows, :]), False)
            va_scr[rows, :LANES] = v_ref[rows, :].astype(BF16)
            va_scr[rows, LANES:] = jnp.ones((tq, LANES), BF16)

    q0 = pl.multiple_of(qi * tq, tq)
    qa_scr[:, :LANES] = q_ref[...].astype(BF16)
    qa_scr[:, LANES:] = _decay_lanes(head_col(cum_ref[pl.ds(q0, tq), :]), True)
    m_scr[...] = jnp.full(m_scr.shape, NEG_INF, F32)
    acc_scr[...] = jnp.zeros(acc_scr.shape, F32)
    p_scr[...] = jnp.zeros(p_scr.shape, BF16)
    alpha_scr[...] = jnp.ones(alpha_scr.shape, F32)

    def flush(kprev):
        k0 = pl.multiple_of(kprev * tq, tq)
        acc_scr[...] = alpha_scr[...] * acc_scr[...] + jnp.dot(p_scr[...], va_scr[pl.ds(k0, tq), :],
                                                              preferred_element_type=F32)

    def tile(ki, masked):
        flush(jnp.maximum(ki - 1, 0))
        k0 = pl.multiple_of(ki * tq, tq)
        s = lax.dot_general(qa_scr[...], ka_scr[pl.ds(k0, tq), :], (((1,), (1,)), ((), ())),
                            preferred_element_type=F32)
        if masked:
            qpos = lax.broadcasted_iota(jnp.int32, (tq, tq), 0)
            kpos = lax.broadcasted_iota(jnp.int32, (tq, tq), 1)
            s = jnp.where(kpos <= qpos, s, NEG_INF)
        m_old = m_scr[...]
        m_new = jnp.maximum(m_old, jnp.max(s, axis=-1, keepdims=True))
        p_scr[...] = jnp.exp2(s * c1 - m_new * c1).astype(BF16)
        alpha_scr[...] = jnp.exp2((m_old - m_new) * c1)
        m_scr[...] = m_new

    def body(ki, carry):
        tile(ki, False)
        return carry

    lax.fori_loop(0, qi, body, 0)
    tile(qi, True)
    flush(qi)
    o_ref[...] = _rms(acc_scr[:, :LANES] / acc_scr[:, LANES:], g_ref[0])


def _attn_prompt(q, k, v, cum_col, g_fox, batch, seq, tq):
    n, width = q.shape
    nh = width // LANES
    nt = seq // tq
    scale = float(LANES) ** -0.5
    return pl.pallas_call(
        functools.partial(_attn_prompt_kernel, scale=scale),
        grid=(batch, nh, nt),
        in_specs=[pl.BlockSpec((tq, LANES), lambda b, h, qi: (b * nt + qi, h)),
                  pl.BlockSpec((seq, LANES), lambda b, h, qi: (b, h)),
                  pl.BlockSpec((seq, LANES), lambda b, h, qi: (b, h)),
                  pl.BlockSpec((seq, nh), lambda b, h, qi: (b, 0)),
                  pl.BlockSpec((1, 1, LANES), lambda b, h, qi: (h, 0, 0))],
        out_specs=pl.BlockSpec((tq, LANES), lambda b, h, qi: (b * nt + qi, h)),
        out_shape=jax.ShapeDtypeStruct((n, width), F32),
        scratch_shapes=[pltpu.VMEM((tq, 2 * LANES), BF16), pltpu.VMEM((seq, 2 * LANES), BF16),
                        pltpu.VMEM((seq, 2 * LANES), BF16), pltpu.VMEM((tq, 1), F32),
                        pltpu.VMEM((tq, 2 * LANES), F32), pltpu.VMEM((tq, tq), BF16), pltpu.VMEM((tq, 1), F32)],
        compiler_params=_params(("parallel", "parallel", "arbitrary")),
        name="attn_prompt",
    )(q, k, v, cum_col, g_fox)


def _attn_sample_kernel(q_ref, kn_ref, vn_ref, kc_ref, vc_ref, cq_ref, ckc_ref, ckn_ref, g_ref, o_ref, *, scale):
    ts, nh = cq_ref.shape
    dn = (((1,), (1,)), ((), ()))
    qpos = lax.broadcasted_iota(jnp.int32, (ts, ts), 0)
    kpos = lax.broadcasted_iota(jnp.int32, (ts, ts), 1)
    for h in range(nh):
        sl = slice(h * LANES, (h + 1) * LANES)
        q = q_ref[:, sl].astype(BF16)
        cq = cq_ref[:, h:h + 1]
        s1 = lax.dot_general(q, kc_ref[:, h, :].astype(BF16), dn, preferred_element_type=F32) * scale
        s1 = s1 + (cq - ckc_ref[h])
        s2 = lax.dot_general(q, kn_ref[:, sl].astype(BF16), dn, preferred_element_type=F32) * scale
        s2 = jnp.where(kpos <= qpos, s2 + (cq - ckn_ref[h]), NEG_INF)
        m = jnp.maximum(jnp.max(s1, axis=-1, keepdims=True), jnp.max(s2, axis=-1, keepdims=True))
        p1 = jnp.exp(s1 - m)
        p2 = jnp.exp(s2 - m)
        l = jnp.sum(p1, axis=-1, keepdims=True) + jnp.sum(p2, axis=-1, keepdims=True)
        o = (jnp.dot(p1.astype(BF16), vc_ref[:, h, :].astype(BF16), preferred_element_type=F32)
             + jnp.dot(p2.astype(BF16), vn_ref[:, sl].astype(BF16), preferred_element_type=F32)) / l
        o_ref[:, sl] = _rms(o, g_ref[:, sl])


def _attn_sample(q, k_new, v_new, k_cache, v_cache, cq_col, ck_cache, ck_new, g_fox_row, ts):
    n, width = q.shape
    nb, past, nh, _ = k_cache.shape
    scale = float(LANES) ** -0.5
    tok = pl.BlockSpec((ts, width), lambda b: (b, 0))
    cache = pl.BlockSpec((None, past, nh, LANES), lambda b: (b, 0, 0, 0))
    return pl.pallas_call(
        functools.partial(_attn_sample_kernel, scale=scale),
        grid=(nb,),
        in_specs=[tok, tok, tok, cache, cache,
                  pl.BlockSpec((ts, nh), lambda b: (b, 0)),
                  pl.BlockSpec((nh, 1, past), lambda b: (b, 0, 0)),
                  pl.BlockSpec((nh, 1, ts), lambda b: (b, 0, 0)),
                  pl.BlockSpec((1, width), lambda b: (0, 0))],
        out_specs=tok,
        out_shape=jax.ShapeDtypeStruct((n, width), F32),
        compiler_params=_params(("parallel",)),
        name="attn_sample",
    )(q, k_new, v_new, k_cache, v_cache, cq_col, ck_cache, ck_new, g_fox_row)


def _merge_kernel(x_ref, fox_ref, us_ref, vs_ref, ws_ref, bs_ref, gs_ref, wo1_ref, wo2_ref, o_ref, cat_scr):
    tm = x_ref.shape[0]
    ng, chunk, _ = ws_ref.shape
    r = lax.broadcasted_iota(jnp.int32, (chunk, chunk), 0)
    c = lax.broadcasted_iota(jnp.int32, (chunk, chunk), 1)
    for g in range(ng):
        sl = slice(g * LANES, (g + 1) * LANES)
        wg = jnp.where(c <= r, ws_ref[g], 0.0).astype(BF16)
        bias = bs_ref[g]
        for j in range(tm // chunk):
            rows = slice(j * chunk, (j + 1) * chunk)
            mixed = jnp.dot(wg, vs_ref[rows, sl].astype(BF16), preferred_element_type=F32) + bias
            cat_scr[rows, sl] = _rms(us_ref[rows, sl] * mixed, gs_ref[:, sl]).astype(BF16)
    o_ref[...] = (x_ref[...]
                  + jnp.dot(fox_ref[...].astype(BF16), wo1_ref[...], preferred_element_type=F32)
                  + jnp.dot(cat_scr[...], wo2_ref[...], preferred_element_type=F32))


def _merge(x2d, fox_n, us, vs, w_s, b_s_col, g_sgu_row, wo1, wo2, tm):
    n, d = x2d.shape
    width = fox_n.shape[1]
    ng, chunk, _ = w_s.shape
    full = lambda shape: pl.BlockSpec(shape, lambda i: (0,) * len(shape))
    tok = pl.BlockSpec((tm, width), lambda i: (i, 0))
    row = pl.BlockSpec((tm, d), lambda i: (i, 0))
    return pl.pallas_call(
        _merge_kernel,
        grid=(n // tm,),
        in_specs=[row, tok, tok, tok, full((ng, chunk, chunk)), full((ng, chunk, 1)), full((1, width)),
                  full((width, d)), full((width, d))],
        out_specs=row,
        out_shape=jax.ShapeDtypeStruct((n, d), F32),
        scratch_shapes=[pltpu.VMEM((tm, width), BF16)],
        compiler_params=_params(("parallel",)),
        name="merge",
    )(x2d, fox_n, us, vs, w_s, b_s_col, g_sgu_row, wo1, wo2)


def _top_rows(s, k, ids=None, payload=None):
    if ids is None:
        ids = lax.broadcasted_iota(jnp.int32, s.shape, 0).astype(F32)
    vals, picks = [], []
    for _ in range(k):
        m = jnp.max(s, axis=0, keepdims=True)
        i = jnp.min(jnp.where(s == m, ids, jnp.inf), axis=0, keepdims=True)
        hit = ids == i
        vals.append(m)
        picks.append(i if payload is None else jnp.max(jnp.where(hit, payload, -1.0), axis=0, keepdims=True))
        s = jnp.where(hit, -jnp.inf, s)
    return jnp.concatenate(vals, axis=0), jnp.concatenate(picks, axis=0)


def _stair_blocks(kk):
    need = {(a, j) for a in range(kk) for j in range(kk) if (a + 1) * (j + 1) <= kk}
    blocks = []
    for a in range(kk):
        nj = kk // (a + 1)
        if nj >= SUBLANES:
            for j0 in range(0, nj, SUBLANES):
                blocks.append(("row", a, j0, 0, min(SUBLANES, nj - j0)))
                need -= {(a, j0 + r) for r in range(SUBLANES)}
    for j in range(kk):
        for a0 in range(0, kk, SUBLANES):
            rs = [r for r in range(SUBLANES) if (a0 + r, j) in need]
            if rs:
                assert rs == list(range(rs[0], rs[-1] + 1))
                blocks.append(("col", a0, j, rs[0], rs[-1] + 1))
                need -= {(a0 + r, j) for r in rs}
    assert not need
    return blocks


def _pair_candidates(sv, si, n_keys):
    kk, cols = sv[0].shape
    sub = lax.broadcasted_iota(jnp.int32, (SUBLANES, cols), 0)
    vals, flat, expert = [], [], []
    for kind, a0, j0, lo, hi in _stair_blocks(kk):
        if kind == "row":
            v = sv[0][a0:a0 + 1] + sv[1][j0:j0 + SUBLANES]
            e = si[0][a0:a0 + 1] * n_keys + si[1][j0:j0 + SUBLANES]
            f = (a0 * kk + j0 + sub).astype(F32)
        else:
            v = sv[0][a0:a0 + SUBLANES] + sv[1][j0:j0 + 1]
            e = si[0][a0:a0 + SUBLANES] * n_keys + si[1][j0:j0 + 1]
            f = ((a0 + sub) * kk + j0).astype(F32)
        if (lo, hi) != (0, SUBLANES):
            v = jnp.where(sub < lo, -jnp.inf, jnp.where(sub < hi, v, -jnp.inf))
        vals.append(v)
        flat.append(f)
        expert.append(e)
    return jnp.concatenate(vals, axis=0), jnp.concatenate(flat, axis=0), jnp.concatenate(expert, axis=0)


def _route_kernel(y_ref, g_ref, wq_ref, sk_ref, h_ref, idx_ref, gate_ref, hb_scr):
    n_keys = sk_ref.shape[2]
    kk = PEER_TOPK

    @pl.when(pl.program_id(1) == 0)
    def _():
        h = _rms(y_ref[...], g_ref[...])
        h_ref[...] = h
        hb_scr[...] = h.astype(BF16)

    q = jnp.dot(hb_scr[...], wq_ref[...], preferred_element_type=F32).astype(BF16)
    dn = (((1,), (1,)), ((), ()))
    sv, si = [], []
    for c in range(2):
        st = lax.dot_general(sk_ref[0, c], q[:, c * LANES:(c + 1) * LANES], dn, preferred_element_type=F32)
        v, i = _top_rows(st, kk)
        sv.append(v)
        si.append(i)
    cand, flat, expert = _pair_candidates(sv, si, n_keys)
    top_s, top_i = _top_rows(cand, kk, ids=flat, payload=expert)
    e = jnp.exp(top_s - top_s[0:1])
    gate_ref[...] = e / jnp.sum(e, axis=0, keepdims=True)
    idx_ref[...] = top_i.astype(jnp.int32)


def _route(y2d, g_row, w_q, sub_keys, tm):
    n, d = y2d.shape
    n_heads, _, n_keys, half = sub_keys.shape
    assert n_keys == LANES and half == LANES
    sel = n_heads * PEER_TOPK
    return pl.pallas_call(
        _route_kernel,
        grid=(n // tm, n_heads),
        in_specs=[pl.BlockSpec((tm, d), lambda i, hd: (i, 0)),
                  pl.BlockSpec((1, d), lambda i, hd: (0, 0)),
                  pl.BlockSpec((d, 2 * half), lambda i, hd: (0, hd)),
                  pl.BlockSpec((1, 2, n_keys, half), lambda i, hd: (hd, 0, 0, 0))],
        out_specs=[pl.BlockSpec((tm, d), lambda i, hd: (i, 0)),
                   pl.BlockSpec((PEER_TOPK, tm), lambda i, hd: (hd, i)),
                   pl.BlockSpec((PEER_TOPK, tm), lambda i, hd: (hd, i))],
        out_shape=[jax.ShapeDtypeStruct((n, d), F32),
                   jax.ShapeDtypeStruct((sel, n), jnp.int32),
                   jax.ShapeDtypeStruct((sel, n), F32)],
        scratch_shapes=[pltpu.VMEM((tm, d), BF16)],
        compiler_params=_params(("parallel", "arbitrary")),
        name="route",
    )(y2d, g_row, w_q, sub_keys)


def _experts_kernel(idx_ref, h_ref, gate_ref, y_ref, uv_ref, o_ref, *scratch):
    tb, d = h_ref.shape
    sel = gate_ref.shape[0]
    bufs, sem = scratch[:-1], scratch[-1]
    nslot = len(bufs)
    ahead = nslot - 1

    def gather(t, slot):
        for j in range(sel):
            e = idx_ref[t * sel + j]
            pltpu.make_async_copy(uv_ref.at[e], bufs[slot].at[pl.ds(j, 1), :], sem.at[slot]).start(priority=j % 2)

    def wait(slot):
        pltpu.make_async_copy(bufs[slot], bufs[slot], sem.at[slot]).wait()

    for t0 in range(ahead):
        gather(t0, t0)
    lane = lax.broadcasted_iota(jnp.int32, gate_ref.shape, 1)
    nch = d // LANES

    def u_part(w):
        return lax.bitcast_convert_type(w << 16, F32)

    def v_part(w):
        return lax.bitcast_convert_type(w & jnp.int32(-65536), F32)

    def group(g, carry):
        base = pl.multiple_of(g * SUBLANES, SUBLANES)
        x8 = h_ref[pl.ds(base, SUBLANES), :]
        rows = []
        for i in range(SUBLANES):
            t = base + i
            slot = i % nslot
            buf = bufs[slot]
            gather(jnp.minimum(t + ahead, tb - 1), (i + ahead) % nslot)
            wait(slot)
            acc = jnp.zeros((sel, LANES), F32)
            for c in range(nch):
                sl = slice(c * LANES, (c + 1) * LANES)
                acc = acc + u_part(buf[:, sl]) * x8[i:i + 1, sl]
            dots = jnp.sum(acc, axis=-1, keepdims=True)
            gate = jnp.sum(jnp.where(lane == t, gate_ref[...], 0.0), axis=-1, keepdims=True)
            act = jnp.broadcast_to(jax.nn.gelu(dots) * gate, (sel, LANES))
            rows.append(jnp.concatenate(
                [jnp.sum(v_part(buf[:, c * LANES:(c + 1) * LANES]) * act, axis=0, keepdims=True)
                 for c in range(nch)], axis=1))
        o_ref[pl.ds(base, SUBLANES), :] = y_ref[pl.ds(base, SUBLANES), :] + jnp.concatenate(rows, axis=0)
        return carry

    lax.fori_loop(0, tb // SUBLANES, group, 0)
    for extra in range(ahead):
        wait((tb + extra) % nslot)


def _experts(idx_flat, h2d, gate_t, y2d, uv, tb):
    n, d = h2d.shape
    sel = gate_t.shape[0]
    assert tb % SUBLANES == 0 and SUBLANES % EXPERT_SLOTS == 0
    return pl.pallas_call(
        _experts_kernel,
        grid=(n // tb,),
        in_specs=[pl.BlockSpec((tb * sel,), lambda i: (i,), memory_space=pltpu.SMEM),
                  pl.BlockSpec((tb, d), lambda i: (i, 0)),
                  pl.BlockSpec((sel, tb), lambda i: (0, i)),
                  pl.BlockSpec((tb, d), lambda i: (i, 0)),
                  pl.BlockSpec(memory_space=pl.ANY)],
        out_specs=pl.BlockSpec((tb, d), lambda i: (i, 0)),
        out_shape=jax.ShapeDtypeStruct((n, d), F32),
        scratch_shapes=[pltpu.VMEM((sel, d), jnp.int32)] * EXPERT_SLOTS + [pltpu.SemaphoreType.DMA((EXPERT_SLOTS,))],
        compiler_params=_params(("arbitrary",)),
        name="experts",
    )(idx_flat, h2d, gate_t, y2d, uv)


def _pack_bf16_pair(lo, hi):
    def bits(x):
        return lax.bitcast_convert_type(lax.reduce_precision(x, exponent_bits=8, mantissa_bits=7), jnp.uint32)
    return lax.bitcast_convert_type((bits(lo) >> 16) | bits(hi), jnp.int32)


def _pick_tile(n, want):
    t = min(n, want)
    assert n % t == 0, (n, t)
    return t


def _layer_front(x2d, p):
    n, _ = x2d.shape
    tm = _pick_tile(n, 512)
    return _inproj(x2d, p["norm_mix_g"], p["w_main"], p["w_f_t"], p["b_f"], p["q_g"], p["k_g"], p["vs_g"], tm)


def _layer_back(x2d, fox_n, us, vs, p, chunk):
    n, _ = x2d.shape
    w_s = p["w_s"][:, :chunk, :chunk]
    b_s = p["b_s"][:, :chunk, None]
    y1 = _merge(x2d, fox_n, us, vs, w_s, b_s, p["g_sgu"], p["wo1"], p["wo2"], _pick_tile(n, 256))
    h2, idx_t, gate_t = _route(y1, p["norm_ffn_g"], p["w_q"], p["sub_keys"], _pick_tile(n, 256))
    idx_flat = idx_t.T.reshape(-1)
    return _experts(idx_flat, h2, gate_t, y1, p["uv"], _pick_tile(n, 128))


def kernel(x_prompt, x_sample, cache_fox_k, cache_fox_v, cache_fox_logf, norm_mix_g, w_in, b_f, q_norm_g, k_norm_g, sgu_v_norm_g, w_s, b_s, fox_out_norm_g, sgu_out_norm_g, w_out, norm_ffn_g, peer_w_q, peer_sub_keys, peer_u, peer_v):
    depth = w_in.shape[0]
    batch, seq, d = x_prompt.shape
    dec_batch, dec_seq, _ = x_sample.shape
    past = cache_fox_k.shape[2]
    nh, dh = cache_fox_k.shape[3], cache_fox_k.shape[4]
    ng, dg = sgu_v_norm_g.shape[1], sgu_v_norm_g.shape[2]
    assert dh == LANES and dg == LANES and nh == ng
    width = nh * dh
    assert w_in.shape[2] == 3 * width + nh + 2 * width

    y_p = x_prompt.reshape(batch * seq, d)
    y_s = x_sample.reshape(dec_batch * dec_seq, d)
    outs = [[] for _ in range(7)]
    for l in range(depth):
        wl = w_in[l]
        p = {
            "norm_mix_g": norm_mix_g[l][None, :],
            "w_main": jnp.concatenate([wl[:, :3 * width], wl[:, 3 * width + nh:]], axis=1).astype(BF16),
            "w_f_t": wl[:, 3 * width:3 * width + nh].T.astype(BF16),
            "b_f": b_f[l][:, None],
            "q_g": jnp.tile(q_norm_g[l], nh)[None, :],
            "k_g": jnp.tile(k_norm_g[l], nh)[None, :],
            "vs_g": sgu_v_norm_g[l].reshape(1, width),
            "w_s": w_s[l],
            "b_s": b_s[l],
            "g_sgu": sgu_out_norm_g[l].reshape(1, width),
            "wo1": w_out[l][:width].astype(BF16),
            "wo2": w_out[l][width:].astype(BF16),
            "norm_ffn_g": norm_ffn_g[l][None, :],
            "w_q": peer_w_q[l].astype(BF16),
            "sub_keys": peer_sub_keys[l].astype(BF16),
            "uv": _pack_bf16_pair(peer_u[l], peer_v[l])[:, None, :],
        }
        g_fox = fox_out_norm_g[l]

        q, k, v, us, vs, lf_t = _layer_front(y_s, p)
        lf_new = lf_t.reshape(nh, dec_batch, dec_seq).transpose(1, 0, 2)
        lf_all = jnp.concatenate([cache_fox_logf[l].astype(F32).transpose(0, 2, 1), lf_new], axis=2)
        total = past + dec_seq
        padded = -(-total // LANES) * LANES
        lf_all = jnp.pad(lf_all, ((0, 0), (0, 0), (0, padded - total))).reshape(dec_batch * nh, padded)
        cum = _cumsum_rows(lf_all).reshape(dec_batch, nh, padded)
        ck_cache = cum[:, :, :past].reshape(dec_batch * nh, 1, past)
        ck_new = cum[:, :, past:total]
        cq_col = ck_new.transpose(0, 2, 1).reshape(dec_batch * dec_seq, nh)
        fox_n = _attn_sample(q, k, v, cache_fox_k[l], cache_fox_v[l], cq_col, ck_cache,
                             ck_new.reshape(dec_batch * nh, 1, dec_seq), g_fox.reshape(1, width), dec_seq)
        y_s = _layer_back(y_s, fox_n, us, vs, p, dec_seq)
        outs[3].append(k.reshape(dec_batch, dec_seq, nh, dh))
        outs[4].append(v.reshape(dec_batch, dec_seq, nh, dh))
        outs[5].append(lf_t.T.reshape(dec_batch, dec_seq, nh))
        outs[6].append(vs.reshape(dec_batch, dec_seq, ng, dg))

        q, k, v, us, vs, lf_t = _layer_front(y_p, p)
        lf_rows = lf_t.reshape(nh, batch, seq).transpose(1, 0, 2).reshape(batch * nh, seq)
        cum = _cumsum_rows(lf_rows)
        cq_col = cum.reshape(batch, nh, seq).transpose(0, 2, 1).reshape(batch * seq, nh)
        fox_n = _attn_prompt(q, k, v, cq_col, g_fox[:, None, :], batch, seq, _pick_tile(seq, 512))
        y_p = _layer_back(y_p, fox_n, us, vs, p, SGU_CHUNK)
        outs[0].append(k.reshape(batch, seq, nh, dh))
        outs[1].append(v.reshape(batch, seq, nh, dh))
        outs[2].append(lf_t.T.reshape(batch, seq, nh))

    return (y_p.reshape(batch, seq, d), y_s.reshape(dec_batch, dec_seq, d)) + tuple(jnp.stack(o) for o in outs)
```

```python
import functools

import jax
import jax.numpy as jnp
from jax import lax
from jax.experimental import pallas as pl
from jax.experimental.pallas import tpu as pltpu

EPS = 1e-6
NEG_INF = -1e30
LANES = 128
SUBLANES = 8
MXU_COLS = 256
PEER_TOPK = 16
SGU_CHUNK = 128
EXPERT_SLOTS = 4
VMEM_LIMIT = 52 * 1024 * 1024

F32 = jnp.float32
BF16 = jnp.bfloat16


def _params(sem):
    return pltpu.CompilerParams(dimension_semantics=sem, vmem_limit_bytes=VMEM_LIMIT)


def _rms(x, g):
    return x * lax.rsqrt(jnp.mean(x * x, axis=-1, keepdims=True) + EPS) * g


def _log_sigmoid(x):
    return jnp.minimum(x, 0.0) - jnp.log1p(jnp.exp(-jnp.abs(x)))


def _inproj_kernel(x_ref, g_ref, w_ref, wf_ref, bf_ref, qg_ref, kg_ref, vg_ref,
                   q_ref, k_ref, v_ref, us_ref, vs_ref, lf_ref, h_scr):
    s = pl.program_id(1)
    width = w_ref.shape[1]

    @pl.when(s == 0)
    def _():
        hb = _rms(x_ref[...], g_ref[...]).astype(BF16)
        h_scr[...] = hb
        fl = lax.dot_general(wf_ref[...], hb, (((1,), (1,)), ((), ())), preferred_element_type=F32)
        lf_ref[...] = _log_sigmoid(fl + bf_ref[...])

    def headwise(o_ref, fn):
        for c0 in range(0, width, MXU_COLS):
            z = jnp.dot(h_scr[...], w_ref[:, c0:c0 + MXU_COLS], preferred_element_type=F32)
            for h0 in range(0, MXU_COLS, LANES):
                sl = slice(c0 + h0, c0 + h0 + LANES)
                o_ref[:, sl] = fn(z[:, h0:h0 + LANES], sl)

    @pl.when(s == 0)
    def _():
        headwise(q_ref, lambda zh, sl: _rms(zh, qg_ref[:, sl]))

    @pl.when(s == 1)
    def _():
        headwise(k_ref, lambda zh, sl: _rms(zh, kg_ref[:, sl]))

    @pl.when(s == 2)
    def _():
        headwise(v_ref, lambda zh, sl: zh)

    @pl.when(s == 3)
    def _():
        headwise(us_ref, lambda zh, sl: jax.nn.gelu(zh))

    @pl.when(s == 4)
    def _():
        headwise(vs_ref, lambda zh, sl: _rms(jax.nn.gelu(zh), vg_ref[:, sl]))


def _inproj(x2d, norm_g, w_main, w_f_t, b_f, q_g, k_g, vs_g, tm):
    n, d = x2d.shape
    width = w_main.shape[1] // 5
    nh = w_f_t.shape[0]
    full = lambda shape: pl.BlockSpec(shape, lambda i, s: (0,) * len(shape))
    tok = pl.BlockSpec((tm, width), lambda i, s: (i, 0))
    out_sds = jax.ShapeDtypeStruct((n, width), F32)
    return pl.pallas_call(
        _inproj_kernel,
        grid=(n // tm, 5),
        in_specs=[pl.BlockSpec((tm, d), lambda i, s: (i, 0)), full((1, d)),
                  pl.BlockSpec((d, width), lambda i, s: (0, s)), full((nh, d)), full((nh, 1)),
                  full((1, width)), full((1, width)), full((1, width))],
        out_specs=[tok, tok, tok, tok, tok, pl.BlockSpec((nh, tm), lambda i, s: (0, i))],
        out_shape=[out_sds] * 5 + [jax.ShapeDtypeStruct((nh, n), F32)],
        scratch_shapes=[pltpu.VMEM((tm, d), BF16)],
        compiler_params=_params(("parallel", "arbitrary")),
        name="inproj",
    )(x2d, norm_g, w_main, w_f_t, b_f, q_g, k_g, vs_g)


def _cumsum_kernel(x_ref, o_ref):
    rows, length = x_ref.shape
    r = lax.broadcasted_iota(jnp.int32, (LANES, LANES), 0)
    c = lax.broadcasted_iota(jnp.int32, (LANES, LANES), 1)
    tri = jnp.where(r <= c, 1.0, 0.0).astype(BF16)
    carry = jnp.zeros((rows, 1), F32)
    for j in range(length // LANES):
        sl = slice(j * LANES, (j + 1) * LANES)
        x = x_ref[:, sl]
        hi = x.astype(BF16)
        r1 = x - hi.astype(F32)
        mid = r1.astype(BF16)
        lo = (r1 - mid.astype(F32)).astype(BF16)
        cs = (jnp.dot(hi, tri, preferred_element_type=F32)
              + jnp.dot(mid, tri, preferred_element_type=F32)
              + jnp.dot(lo, tri, preferred_element_type=F32))
        out = cs + carry
        o_ref[:, sl] = out
        carry = out[:, LANES - 1:LANES]


def _cumsum_rows(x):
    rows, length = x.shape
    rb = _pick_tile(rows, 8)
    return pl.pallas_call(
        _cumsum_kernel,
        grid=(rows // rb,),
        in_specs=[pl.BlockSpec((rb, length), lambda i: (i, 0))],
        out_specs=pl.BlockSpec((rb, length), lambda i: (i, 0)),
        out_shape=jax.ShapeDtypeStruct((rows, length), F32),
        compiler_params=_params(("parallel",)),
        name="cumsum",
    )(x)


def _split3(x):
    hi = x.astype(BF16).astype(F32)
    r = x - hi
    mid = r.astype(BF16).astype(F32)
    return hi, mid, (r - mid).astype(BF16).astype(F32)


def _decay_lanes(c, query_side):
    rows = c.shape[0]
    lane = lax.broadcasted_iota(jnp.int32, (rows, LANES), 1)
    hi, mid, lo = _split3(c if query_side else -c)
    if query_side:
        out = jnp.where(lane == 0, hi, jnp.where(lane == 1, mid, jnp.where(lane == 2, lo,
                                                                          jnp.where(lane < 6, 1.0, 0.0))))
    else:
        out = jnp.where(lane < 3, 1.0, jnp.where(lane == 3, hi, jnp.where(lane == 4, mid,
                                                                         jnp.where(lane == 5, lo, 0.0))))
    return out.astype(BF16)


def _attn_prompt_kernel(q_ref, k_ref, v_ref, cum_ref, g_ref, o_ref, qa_scr, ka_scr, va_scr, m_scr, acc_scr,
                        p_scr, alpha_scr, *, scale):
    h = pl.program_id(1)
    qi = pl.program_id(2)
    tq = q_ref.shape[0]
    seq = k_ref.shape[0]
    nh = cum_ref.shape[1]
    c1 = scale * 1.4426950408889634
    head_lane = lax.broadcasted_iota(jnp.int32, (tq, nh), 1)

    def head_col(x):
        return jnp.sum(jnp.where(head_lane == h, x, 0.0), axis=-1, keepdims=True) * (1.0 / scale)

    @pl.when(qi == 0)
    def _():
        for j in range(seq // tq):
            rows = slice(j * tq, (j + 1) * tq)
            ka_scr[rows, :LANES] = k_ref[rows, :].astype(BF16)
            ka_scr[rows, LANES:] = _decay_lanes(head_col(cum_ref[rows, :]), False)
            va_scr[rows, :LANES] = v_ref[rows, :].astype(BF16)
            va_scr[rows, LANES:] = jnp.ones((tq, LANES), BF16)

    q0 = pl.multiple_of(qi * tq, tq)
    qa_scr[:, :LANES] = q_ref[...].astype(BF16)
    qa_scr[:, LANES:] = _decay_lanes(head_col(cum_ref[pl.ds(q0, tq), :]), True)
    m_scr[...] = jnp.full(m_scr.shape, NEG_INF, F32)
    acc_scr[...] = jnp.zeros(acc_scr.shape, F32)
    p_scr[...] = jnp.zeros(p_scr.shape, BF16)
    alpha_scr[...] = jnp.ones(alpha_scr.shape, F32)

    def flush(kprev):
        k0 = pl.multiple_of(kprev * tq, tq)
        acc_scr[...] = alpha_scr[...] * acc_scr[...] + jnp.dot(p_scr[...], va_scr[pl.ds(k0, tq), :],
                                                              preferred_element_type=F32)

    def tile(ki, masked):
        flush(jnp.maximum(ki - 1, 0))
        k0 = pl.multiple_of(ki * tq, tq)
        s = lax.dot_general(qa_scr[...], ka_scr[pl.ds(k0, tq), :], (((1,), (1,)), ((), ())),
                            preferred_element_type=F32)
        if masked:
            qpos = lax.broadcasted_iota(jnp.int32, (tq, tq), 0)
            kpos = lax.broadcasted_iota(jnp.int32, (tq, tq), 1)
            s = jnp.where(kpos <= qpos, s, NEG_INF)
        m_old = m_scr[...]
        m_new = jnp.maximum(m_old, jnp.max(s, axis=-1, keepdims=True))
        p_scr[...] = jnp.exp2(s * c1 - m_new * c1).astype(BF16)
        alpha_scr[...] = jnp.exp2((m_old - m_new) * c1)
        m_scr[...] = m_new

    def body(ki, carry):
        tile(ki, False)
        return carry

    lax.fori_loop(0, qi, body, 0)
    tile(qi, True)
    flush(qi)
    o_ref[...] = _rms(acc_scr[:, :LANES] / acc_scr[:, LANES:], g_ref[0])


def _attn_prompt(q, k, v, cum_col, g_fox, batch, seq, tq):
    n, width = q.shape
    nh = width // LANES
    nt = seq // tq
    scale = float(LANES) ** -0.5
    return pl.pallas_call(
        functools.partial(_attn_prompt_kernel, scale=scale),
        grid=(batch, nh, nt),
        in_specs=[pl.BlockSpec((tq, LANES), lambda b, h, qi: (b * nt + qi, h)),
                  pl.BlockSpec((seq, LANES), lambda b, h, qi: (b, h)),
                  pl.BlockSpec((seq, LANES), lambda b, h, qi: (b, h)),
                  pl.BlockSpec((seq, nh), lambda b, h, qi: (b, 0)),
                  pl.BlockSpec((1, 1, LANES), lambda b, h, qi: (h, 0, 0))],
        out_specs=pl.BlockSpec((tq, LANES), lambda b, h, qi: (b * nt + qi, h)),
        out_shape=jax.ShapeDtypeStruct((n, width), F32),
        scratch_shapes=[pltpu.VMEM((tq, 2 * LANES), BF16), pltpu.VMEM((seq, 2 * LANES), BF16),
                        pltpu.VMEM((seq, 2 * LANES), BF16), pltpu.VMEM((tq, 1), F32),
                        pltpu.VMEM((tq, 2 * LANES), F32), pltpu.VMEM((tq, tq), BF16), pltpu.VMEM((tq, 1), F32)],
        compiler_params=_params(("parallel", "parallel", "arbitrary")),
        name="attn_prompt",
    )(q, k, v, cum_col, g_fox)


def _attn_sample_kernel(q_ref, kn_ref, vn_ref, kc_ref, vc_ref, cq_ref, ckc_ref, ckn_ref, g_ref, o_ref, *, scale):
    ts, nh = cq_ref.shape
    dn = (((1,), (1,)), ((), ()))
    qpos = lax.broadcasted_iota(jnp.int32, (ts, ts), 0)
    kpos = lax.broadcasted_iota(jnp.int32, (ts, ts), 1)
    for h in range(nh):
        sl = slice(h * LANES, (h + 1) * LANES)
        q = q_ref[:, sl].astype(BF16)
        cq = cq_ref[:, h:h + 1]
        s1 = lax.dot_general(q, kc_ref[:, h, :].astype(BF16), dn, preferred_element_type=F32) * scale
        s1 = s1 + (cq - ckc_ref[h])
        s2 = lax.dot_general(q, kn_ref[:, sl].astype(BF16), dn, preferred_element_type=F32) * scale
        s2 = jnp.where(kpos <= qpos, s2 + (cq - ckn_ref[h]), NEG_INF)
        m = jnp.maximum(jnp.max(s1, axis=-1, keepdims=True), jnp.max(s2, axis=-1, keepdims=True))
        p1 = jnp.exp(s1 - m)
        p2 = jnp.exp(s2 - m)
        l = jnp.sum(p1, axis=-1, keepdims=True) + jnp.sum(p2, axis=-1, keepdims=True)
        o = (jnp.dot(p1.astype(BF16), vc_ref[:, h, :].astype(BF16), preferred_element_type=F32)
             + jnp.dot(p2.astype(BF16), vn_ref[:, sl].astype(BF16), preferred_element_type=F32)) / l
        o_ref[:, sl] = _rms(o, g_ref[:, sl])


def _attn_sample(q, k_new, v_new, k_cache, v_cache, cq_col, ck_cache, ck_new, g_fox_row, ts):
    n, width = q.shape
    nb, past, nh, _ = k_cache.shape
    scale = float(LANES) ** -0.5
    tok = pl.BlockSpec((ts, width), lambda b: (b, 0))
    cache = pl.BlockSpec((None, past, nh, LANES), lambda b: (b, 0, 0, 0))
    return pl.pallas_call(
        functools.partial(_attn_sample_kernel, scale=scale),
        grid=(nb,),
        in_specs=[tok, tok, tok, cache, cache,
                  pl.BlockSpec((ts, nh), lambda b: (b, 0)),
                  pl.BlockSpec((nh, 1, past), lambda b: (b, 0, 0)),
                  pl.BlockSpec((nh, 1, ts), lambda b: (b, 0, 0)),
                  pl.BlockSpec((1, width), lambda b: (0, 0))],
        out_specs=tok,
        out_shape=jax.ShapeDtypeStruct((n, width), F32),
        compiler_params=_params(("parallel",)),
        name="attn_sample",
    )(q, k_new, v_new, k_cache, v_cache, cq_col, ck_cache, ck_new, g_fox_row)


def _merge_kernel(x_ref, fox_ref, us_ref, vs_ref, ws_ref, bs_ref, gs_ref, wo1_ref, wo2_ref, o_ref, cat_scr):
    tm = x_ref.shape[0]
    ng, chunk, _ = ws_ref.shape
    r = lax.broadcasted_iota(jnp.int32, (chunk, chunk), 0)
    c = lax.broadcasted_iota(jnp.int32, (chunk, chunk), 1)
    for g in range(ng):
        sl = slice(g * LANES, (g + 1) * LANES)
        wg = jnp.where(c <= r, ws_ref[g], 0.0).astype(BF16)
        bias = bs_ref[g]
        for j in range(tm // chunk):
            rows = slice(j * chunk, (j + 1) * chunk)
            mixed = jnp.dot(wg, vs_ref[rows, sl].astype(BF16), preferred_element_type=F32) + bias
            cat_scr[rows, sl] = _rms(us_ref[rows, sl] * mixed, gs_ref[:, sl]).astype(BF16)
    o_ref[...] = (x_ref[...]
                  + jnp.dot(fox_ref[...].astype(BF16), wo1_ref[...], preferred_element_type=F32)
                  + jnp.dot(cat_scr[...], wo2_ref[...], preferred_element_type=F32))


def _merge(x2d, fox_n, us, vs, w_s, b_s_col, g_sgu_row, wo1, wo2, tm):
    n, d = x2d.shape
    width = fox_n.shape[1]
    ng, chunk, _ = w_s.shape
    full = lambda shape: pl.BlockSpec(shape, lambda i: (0,) * len(shape))
    tok = pl.BlockSpec((tm, width), lambda i: (i, 0))
    row = pl.BlockSpec((tm, d), lambda i: (i, 0))
    return pl.pallas_call(
        _merge_kernel,
        grid=(n // tm,),
        in_specs=[row, tok, tok, tok, full((ng, chunk, chunk)), full((ng, chunk, 1)), full((1, width)),
                  full((width, d)), full((width, d))],
        out_specs=row,
        out_shape=jax.ShapeDtypeStruct((n, d), F32),
        scratch_shapes=[pltpu.VMEM((tm, width), BF16)],
        compiler_params=_params(("parallel",)),
        name="merge",
    )(x2d, fox_n, us, vs, w_s, b_s_col, g_sgu_row, wo1, wo2)


def _top_rows(s, k, ids=None, payload=None):
    if ids is None:
        ids = lax.broadcasted_iota(jnp.int32, s.shape, 0).astype(F32)
    vals, picks = [], []
    for _ in range(k):
        m = jnp.max(s, axis=0, keepdims=True)
        i = jnp.min(jnp.where(s == m, ids, jnp.inf), axis=0, keepdims=True)
        hit = ids == i
        vals.append(m)
        picks.append(i if payload is None else jnp.max(jnp.where(hit, payload, -1.0), axis=0, keepdims=True))
        s = jnp.where(hit, -jnp.inf, s)
    return jnp.concatenate(vals, axis=0), jnp.concatenate(picks, axis=0)


def _stair_blocks(kk):
    need = {(a, j) for a in range(kk) for j in range(kk) if (a + 1) * (j + 1) <= kk}
    blocks = []
    for a in range(kk):
        nj = kk // (a + 1)
        if nj >= SUBLANES:
            for j0 in range(0, nj, SUBLANES):
                blocks.append(("row", a, j0, 0, min(SUBLANES, nj - j0)))
                need -= {(a, j0 + r) for r in range(SUBLANES)}
    for j in range(kk):
        for a0 in range(0, kk, SUBLANES):
            rs = [r for r in range(SUBLANES) if (a0 + r, j) in need]
            if rs:
                assert rs == list(range(rs[0], rs[-1] + 1))
                blocks.append(("col", a0, j, rs[0], rs[-1] + 1))
                need -= {(a0 + r, j) for r in rs}
    assert not need
    return blocks


def _pair_candidates(sv, si, n_keys):
    kk, cols = sv[0].shape
    sub = lax.broadcasted_iota(jnp.int32, (SUBLANES, cols), 0)
    vals, flat, expert = [], [], []
    for kind, a0, j0, lo, hi in _stair_blocks(kk):
        if kind == "row":
            v = sv[0][a0:a0 + 1] + sv[1][j0:j0 + SUBLANES]
            e = si[0][a0:a0 + 1] * n_keys + si[1][j0:j0 + SUBLANES]
            f = (a0 * kk + j0 + sub).astype(F32)
        else:
            v = sv[0][a0:a0 + SUBLANES] + sv[1][j0:j0 + 1]
            e = si[0][a0:a0 + SUBLANES] * n_keys + si[1][j0:j0 + 1]
            f = ((a0 + sub) * kk + j0).astype(F32)
        if (lo, hi) != (0, SUBLANES):
            v = jnp.where(sub < lo, -jnp.inf, jnp.where(sub < hi, v, -jnp.inf))
        vals.append(v)
        flat.append(f)
        expert.append(e)
    return jnp.concatenate(vals, axis=0), jnp.concatenate(flat, axis=0), jnp.concatenate(expert, axis=0)


def _route_kernel(y_ref, g_ref, wq_ref, sk_ref, h_ref, idx_ref, gate_ref, hb_scr):
    n_keys = sk_ref.shape[2]
    kk = PEER_TOPK

    @pl.when(pl.program_id(1) == 0)
    def _():
        h = _rms(y_ref[...], g_ref[...])
        h_ref[...] = h
        hb_scr[...] = h.astype(BF16)

    q = jnp.dot(hb_scr[...], wq_ref[...], preferred_element_type=F32).astype(BF16)
    dn = (((1,), (1,)), ((), ()))
    sv, si = [], []
    for c in range(2):
        st = lax.dot_general(sk_ref[0, c], q[:, c * LANES:(c + 1) * LANES], dn, preferred_element_type=F32)
        v, i = _top_rows(st, kk)
        sv.append(v)
        si.append(i)
    cand, flat, expert = _pair_candidates(sv, si, n_keys)
    top_s, top_i = _top_rows(cand, kk, ids=flat, payload=expert)
    e = jnp.exp(top_s - top_s[0:1])
    gate_ref[...] = e / jnp.sum(e, axis=0, keepdims=True)
    idx_ref[...] = top_i.astype(jnp.int32)


def _route(y2d, g_row, w_q, sub_keys, tm):
    n, d = y2d.shape
    n_heads, _, n_keys, half = sub_keys.shape
    assert n_keys == LANES and half == LANES
    sel = n_heads * PEER_TOPK
    return pl.pallas_call(
        _route_kernel,
        grid=(n // tm, n_heads),
        in_specs=[pl.BlockSpec((tm, d), lambda i, hd: (i, 0)),
                  pl.BlockSpec((1, d), lambda i, hd: (0, 0)),
                  pl.BlockSpec((d, 2 * half), lambda i, hd: (0, hd)),
                  pl.BlockSpec((1, 2, n_keys, half), lambda i, hd: (hd, 0, 0, 0))],
        out_specs=[pl.BlockSpec((tm, d), lambda i, hd: (i, 0)),
                   pl.BlockSpec((PEER_TOPK, tm), lambda i, hd: (hd, i)),
                   pl.BlockSpec((PEER_TOPK, tm), lambda i, hd: (hd, i))],
        out_shape=[jax.ShapeDtypeStruct((n, d), F32),
                   jax.ShapeDtypeStruct((sel, n), jnp.int32),
                   jax.ShapeDtypeStruct((sel, n), F32)],
        scratch_shapes=[pltpu.VMEM((tm, d), BF16)],
        compiler_params=_params(("parallel", "arbitrary")),
        name="route",
    )(y2d, g_row, w_q, sub_keys)


def _experts_kernel(idx_ref, h_ref, gate_ref, y_ref, uv_ref, o_ref, *scratch):
    tb, d = h_ref.shape
    sel = gate_ref.shape[0]
    bufs, sem = scratch[:-1], scratch[-1]
    nslot = len(bufs)
    ahead = nslot - 1

    def gather(t, slot):
        for j in range(sel):
            e = idx_ref[t * sel + j]
            pltpu.make_async_copy(uv_ref.at[e], bufs[slot].at[pl.ds(j, 1), :], sem.at[slot]).start(priority=j % 2)

    def wait(slot):
        pltpu.make_async_copy(bufs[slot], bufs[slot], sem.at[slot]).wait()

    for t0 in range(ahead):
        gather(t0, t0)
    lane = lax.broadcasted_iota(jnp.int32, gate_ref.shape, 1)
    nch = d // LANES

    def u_part(w):
        return lax.bitcast_convert_type(w << 16, F32)

    def v_part(w):
        return lax.bitcast_convert_type(w & jnp.int32(-65536), F32)

    def group(g, carry):
        base = pl.multiple_of(g * SUBLANES, SUBLANES)
        x8 = h_ref[pl.ds(base, SUBLANES), :]
        rows = []
        for i in range(SUBLANES):
            t = base + i
            slot = i % nslot
            buf = bufs[slot]
            gather(jnp.minimum(t + ahead, tb - 1), (i + ahead) % nslot)
            wait(slot)
            acc = jnp.zeros((sel, LANES), F32)
            for c in range(nch):
                sl = slice(c * LANES, (c + 1) * LANES)
                acc = acc + u_part(buf[:, sl]) * x8[i:i + 1, sl]
            dots = jnp.sum(acc, axis=-1, keepdims=True)
            gate = jnp.sum(jnp.where(lane == t, gate_ref[...], 0.0), axis=-1, keepdims=True)
            act = jnp.broadcast_to(jax.nn.gelu(dots) * gate, (sel, LANES))
            rows.append(jnp.concatenate(
                [jnp.sum(v_part(buf[:, c * LANES:(c + 1) * LANES]) * act, axis=0, keepdims=True)
                 for c in range(nch)], axis=1))
        o_ref[pl.ds(base, SUBLANES), :] = y_ref[pl.ds(base, SUBLANES), :] + jnp.concatenate(rows, axis=0)
        return carry

    lax.fori_loop(0, tb // SUBLANES, group, 0)
    for extra in range(ahead):
        wait((tb + extra) % nslot)


def _experts(idx_flat, h2d, gate_t, y2d, uv, tb):
    n, d = h2d.shape
    sel = gate_t.shape[0]
    assert tb % SUBLANES == 0 and SUBLANES % EXPERT_SLOTS == 0
    return pl.pallas_call(
        _experts_kernel,
        grid=(n // tb,),
        in_specs=[pl.BlockSpec((tb * sel,), lambda i: (i,), memory_space=pltpu.SMEM),
                  pl.BlockSpec((tb, d), lambda i: (i, 0)),
                  pl.BlockSpec((sel, tb), lambda i: (0, i)),
                  pl.BlockSpec((tb, d), lambda i: (i, 0)),
                  pl.BlockSpec(memory_space=pl.ANY)],
        out_specs=pl.BlockSpec((tb, d), lambda i: (i, 0)),
        out_shape=jax.ShapeDtypeStruct((n, d), F32),
        scratch_shapes=[pltpu.VMEM((sel, d), jnp.int32)] * EXPERT_SLOTS + [pltpu.SemaphoreType.DMA((EXPERT_SLOTS,))],
        compiler_params=_params(("arbitrary",)),
        name="experts",
    )(idx_flat, h2d, gate_t, y2d, uv)


def _pack_bf16_pair(lo, hi):
    def bits(x):
        return lax.bitcast_convert_type(lax.reduce_precision(x, exponent_bits=8, mantissa_bits=7), jnp.uint32)
    return lax.bitcast_convert_type((bits(lo) >> 16) | bits(hi), jnp.int32)


def _pick_tile(n, want):
    t = min(n, want)
    assert n % t == 0, (n, t)
    return t


def _layer_front(x2d, p):
    n, _ = x2d.shape
    tm = _pick_tile(n, 512)
    return _inproj(x2d, p["norm_mix_g"], p["w_main"], p["w_f_t"], p["b_f"], p["q_g"], p["k_g"], p["vs_g"], tm)


def _layer_back(x2d, fox_n, us, vs, p, chunk):
    n, _ = x2d.shape
    w_s = p["w_s"][:, :chunk, :chunk]
    b_s = p["b_s"][:, :chunk, None]
    y1 = _merge(x2d, fox_n, us, vs, w_s, b_s, p["g_sgu"], p["wo1"], p["wo2"], _pick_tile(n, 256))
    h2, idx_t, gate_t = _route(y1, p["norm_ffn_g"], p["w_q"], p["sub_keys"], _pick_tile(n, 256))
    idx_flat = idx_t.T.reshape(-1)
    return _experts(idx_flat, h2, gate_t, y1, p["uv"], _pick_tile(n, 512))


def kernel(x_prompt, x_sample, cache_fox_k, cache_fox_v, cache_fox_logf, norm_mix_g, w_in, b_f, q_norm_g, k_norm_g, sgu_v_norm_g, w_s, b_s, fox_out_norm_g, sgu_out_norm_g, w_out, norm_ffn_g, peer_w_q, peer_sub_keys, peer_u, peer_v):
    depth = w_in.shape[0]
    batch, seq, d = x_prompt.shape
    dec_batch, dec_seq, _ = x_sample.shape
    past = cache_fox_k.shape[2]
    nh, dh = cache_fox_k.shape[3], cache_fox_k.shape[4]
    ng, dg = sgu_v_norm_g.shape[1], sgu_v_norm_g.shape[2]
    assert dh == LANES and dg == LANES and nh == ng
    width = nh * dh
    assert w_in.shape[2] == 3 * width + nh + 2 * width

    y_p = x_prompt.reshape(batch * seq, d)
    y_s = x_sample.reshape(dec_batch * dec_seq, d)
    outs = [[] for _ in range(7)]
    for l in range(depth):
        wl = w_in[l]
        p = {
            "norm_mix_g": norm_mix_g[l][None, :],
            "w_main": jnp.concatenate([wl[:, :3 * width], wl[:, 3 * width + nh:]], axis=1).astype(BF16),
            "w_f_t": wl[:, 3 * width:3 * width + nh].T.astype(BF16),
            "b_f": b_f[l][:, None],
            "q_g": jnp.tile(q_norm_g[l], nh)[None, :],
            "k_g": jnp.tile(k_norm_g[l], nh)[None, :],
            "vs_g": sgu_v_norm_g[l].reshape(1, width),
            "w_s": w_s[l],
            "b_s": b_s[l],
            "g_sgu": sgu_out_norm_g[l].reshape(1, width),
            "wo1": w_out[l][:width].astype(BF16),
            "wo2": w_out[l][width:].astype(BF16),
            "norm_ffn_g": norm_ffn_g[l][None, :],
            "w_q": peer_w_q[l].astype(BF16),
            "sub_keys": peer_sub_keys[l].astype(BF16),
            "uv": _pack_bf16_pair(peer_u[l], peer_v[l])[:, None, :],
        }
        g_fox = fox_out_norm_g[l]

        q, k, v, us, vs, lf_t = _layer_front(y_s, p)
        lf_new = lf_t.reshape(nh, dec_batch, dec_seq).transpose(1, 0, 2)
        lf_all = jnp.concatenate([cache_fox_logf[l].astype(F32).transpose(0, 2, 1), lf_new], axis=2)
        total = past + dec_seq
        padded = -(-total // LANES) * LANES
        lf_all = jnp.pad(lf_all, ((0, 0), (0, 0), (0, padded - total))).reshape(dec_batch * nh, padded)
        cum = _cumsum_rows(lf_all).reshape(dec_batch, nh, padded)
        ck_cache = cum[:, :, :past].reshape(dec_batch * nh, 1, past)
        ck_new = cum[:, :, past:total]
        cq_col = ck_new.transpose(0, 2, 1).reshape(dec_batch * dec_seq, nh)
        fox_n = _attn_sample(q, k, v, cache_fox_k[l], cache_fox_v[l], cq_col, ck_cache,
                             ck_new.reshape(dec_batch * nh, 1, dec_seq), g_fox.reshape(1, width), dec_seq)
        y_s = _layer_back(y_s, fox_n, us, vs, p, dec_seq)
        outs[3].append(k.reshape(dec_batch, dec_seq, nh, dh))
        outs[4].append(v.reshape(dec_batch, dec_seq, nh, dh))
        outs[5].append(lf_t.T.reshape(dec_batch, dec_seq, nh))
        outs[6].append(vs.reshape(dec_batch, dec_seq, ng, dg))

        q, k, v, us, vs, lf_t = _layer_front(y_p, p)
        lf_rows = lf_t.reshape(nh, batch, seq).transpose(1, 0, 2).reshape(batch * nh, seq)
        cum = _cumsum_rows(lf_rows)
        cq_col = cum.reshape(batch, nh, seq).transpose(0, 2, 1).reshape(batch * seq, nh)
        fox_n = _attn_prompt(q, k, v, cq_col, g_fox[:, None, :], batch, seq, _pick_tile(seq, 512))
        y_p = _layer_back(y_p, fox_n, us, vs, p, SGU_CHUNK)
        outs[0].append(k.reshape(batch, seq, nh, dh))
        outs[1].append(v.reshape(batch, seq, nh, dh))
        outs[2].append(lf_t.T.reshape(batch, seq, nh))

    return (y_p.reshape(batch, seq, d), y_s.reshape(dec_batch, dec_seq, d)) + tuple(jnp.stack(o) for o in outs)
```

```python
import functools

import jax
import jax.numpy as jnp
from jax import lax
from jax.experimental import pallas as pl
from jax.experimental.pallas import tpu as pltpu

EPS = 1e-6
NEG_INF = -1e30
LANES = 128
SUBLANES = 8
MXU_COLS = 256
PEER_TOPK = 16
SGU_CHUNK = 128
EXPERT_SLOTS = 4
VMEM_LIMIT = 52 * 1024 * 1024

F32 = jnp.float32
BF16 = jnp.bfloat16


def _params(sem):
    return pltpu.CompilerParams(dimension_semantics=sem, vmem_limit_bytes=VMEM_LIMIT)


def _rms(x, g):
    return x * lax.rsqrt(jnp.mean(x * x, axis=-1, keepdims=True) + EPS) * g


def _log_sigmoid(x):
    return jnp.minimum(x, 0.0) - jnp.log1p(jnp.exp(-jnp.abs(x)))


def _inproj_kernel(x_ref, g_ref, w_ref, wf_ref, bf_ref, qg_ref, kg_ref, vg_ref,
                   q_ref, k_ref, v_ref, us_ref, vs_ref, lf_ref, h_scr):
    s = pl.program_id(1)
    width = w_ref.shape[1]

    @pl.when(s == 0)
    def _():
        hb = _rms(x_ref[...], g_ref[...]).astype(BF16)
        h_scr[...] = hb
        fl = lax.dot_general(wf_ref[...], hb, (((1,), (1,)), ((), ())), preferred_element_type=F32)
        lf_ref[...] = _log_sigmoid(fl + bf_ref[...])

    def headwise(o_ref, fn):
        for c0 in range(0, width, MXU_COLS):
            z = jnp.dot(h_scr[...], w_ref[:, c0:c0 + MXU_COLS], preferred_element_type=F32)
            for h0 in range(0, MXU_COLS, LANES):
                sl = slice(c0 + h0, c0 + h0 + LANES)
                o_ref[:, sl] = fn(z[:, h0:h0 + LANES], sl)

    @pl.when(s == 0)
    def _():
        headwise(q_ref, lambda zh, sl: _rms(zh, qg_ref[:, sl]))

    @pl.when(s == 1)
    def _():
        headwise(k_ref, lambda zh, sl: _rms(zh, kg_ref[:, sl]))

    @pl.when(s == 2)
    def _():
        headwise(v_ref, lambda zh, sl: zh)

    @pl.when(s == 3)
    def _():
        headwise(us_ref, lambda zh, sl: jax.nn.gelu(zh))

    @pl.when(s == 4)
    def _():
        headwise(vs_ref, lambda zh, sl: _rms(jax.nn.gelu(zh), vg_ref[:, sl]))


def _inproj(x2d, norm_g, w_main, w_f_t, b_f, q_g, k_g, vs_g, tm):
    n, d = x2d.shape
    width = w_main.shape[1] // 5
    nh = w_f_t.shape[0]
    full = lambda shape: pl.BlockSpec(shape, lambda i, s: (0,) * len(shape))
    tok = pl.BlockSpec((tm, width), lambda i, s: (i, 0))
    out_sds = jax.ShapeDtypeStruct((n, width), F32)
    return pl.pallas_call(
        _inproj_kernel,
        grid=(n // tm, 5),
        in_specs=[pl.BlockSpec((tm, d), lambda i, s: (i, 0)), full((1, d)),
                  pl.BlockSpec((d, width), lambda i, s: (0, s)), full((nh, d)), full((nh, 1)),
                  full((1, width)), full((1, width)), full((1, width))],
        out_specs=[tok, tok, tok, tok, tok, pl.BlockSpec((nh, tm), lambda i, s: (0, i))],
        out_shape=[out_sds] * 5 + [jax.ShapeDtypeStruct((nh, n), F32)],
        scratch_shapes=[pltpu.VMEM((tm, d), BF16)],
        compiler_params=_params(("parallel", "arbitrary")),
        name="inproj",
    )(x2d, norm_g, w_main, w_f_t, b_f, q_g, k_g, vs_g)


def _cumsum_kernel(x_ref, o_ref):
    rows, length = x_ref.shape
    r = lax.broadcasted_iota(jnp.int32, (LANES, LANES), 0)
    c = lax.broadcasted_iota(jnp.int32, (LANES, LANES), 1)
    tri = jnp.where(r <= c, 1.0, 0.0).astype(BF16)
    carry = jnp.zeros((rows, 1), F32)
    for j in range(length // LANES):
        sl = slice(j * LANES, (j + 1) * LANES)
        x = x_ref[:, sl]
        hi = x.astype(BF16)
        r1 = x - hi.astype(F32)
        mid = r1.astype(BF16)
        lo = (r1 - mid.astype(F32)).astype(BF16)
        cs = (jnp.dot(hi, tri, preferred_element_type=F32)
              + jnp.dot(mid, tri, preferred_element_type=F32)
              + jnp.dot(lo, tri, preferred_element_type=F32))
        out = cs + carry
        o_ref[:, sl] = out
        carry = out[:, LANES - 1:LANES]


def _cumsum_rows(x):
    rows, length = x.shape
    rb = _pick_tile(rows, 8)
    return pl.pallas_call(
        _cumsum_kernel,
        grid=(rows // rb,),
        in_specs=[pl.BlockSpec((rb, length), lambda i: (i, 0))],
        out_specs=pl.BlockSpec((rb, length), lambda i: (i, 0)),
        out_shape=jax.ShapeDtypeStruct((rows, length), F32),
        compiler_params=_params(("parallel",)),
        name="cumsum",
    )(x)


def _split3(x):
    hi = x.astype(BF16).astype(F32)
    r = x - hi
    mid = r.astype(BF16).astype(F32)
    return hi, mid, (r - mid).astype(BF16).astype(F32)


def _decay_lanes(c, query_side):
    rows = c.shape[0]
    lane = lax.broadcasted_iota(jnp.int32, (rows, LANES), 1)
    hi, mid, lo = _split3(c if query_side else -c)
    if query_side:
        out = jnp.where(lane == 0, hi, jnp.where(lane == 1, mid, jnp.where(lane == 2, lo,
                                                                          jnp.where(lane < 6, 1.0, 0.0))))
    else:
        out = jnp.where(lane < 3, 1.0, jnp.where(lane == 3, hi, jnp.where(lane == 4, mid,
                                                                         jnp.where(lane == 5, lo, 0.0))))
    return out.astype(BF16)


def _attn_prompt_kernel(q_ref, k_ref, v_ref, cum_ref, g_ref, o_ref, qa_scr, ka_scr, va_scr, m_scr, acc_scr,
                        p_scr, alpha_scr, *, scale):
    h = pl.program_id(1)
    qi = pl.program_id(2)
    tq = q_ref.shape[0]
    seq = k_ref.shape[0]
    nh = cum_ref.shape[1]
    c1 = scale * 1.4426950408889634
    head_lane = lax.broadcasted_iota(jnp.int32, (tq, nh), 1)

    def head_col(x):
        return jnp.sum(jnp.where(head_lane == h, x, 0.0), axis=-1, keepdims=True) * (1.0 / scale)

    @pl.when(qi == 0)
    def _():
        for j in range(seq // tq):
            rows = slice(j * tq, (j + 1) * tq)
            ka_scr[rows, :LANES] = k_ref[rows, :].astype(BF16)
            ka_scr[rows, LANES:] = _decay_lanes(head_col(cum_ref[rows, :]), False)
            va_scr[rows, :LANES] = v_ref[rows, :].astype(BF16)
            va_scr[rows, LANES:] = jnp.ones((tq, LANES), BF16)

    q0 = pl.multiple_of(qi * tq, tq)
    qa_scr[:, :LANES] = q_ref[...].astype(BF16)
    qa_scr[:, LANES:] = _decay_lanes(head_col(cum_ref[pl.ds(q0, tq), :]), True)
    m_scr[...] = jnp.full(m_scr.shape, NEG_INF, F32)
    acc_scr[...] = jnp.zeros(acc_scr.shape, F32)
    p_scr[...] = jnp.zeros(p_scr.shape, BF16)
    alpha_scr[...] = jnp.ones(alpha_scr.shape, F32)

    def flush(kprev):
        k0 = pl.multiple_of(kprev * tq, tq)
        acc_scr[...] = alpha_scr[...] * acc_scr[...] + jnp.dot(p_scr[...], va_scr[pl.ds(k0, tq), :],
                                                              preferred_element_type=F32)

    def tile(ki, masked):
        flush(jnp.maximum(ki - 1, 0))
        k0 = pl.multiple_of(ki * tq, tq)
        s = lax.dot_general(qa_scr[...], ka_scr[pl.ds(k0, tq), :], (((1,), (1,)), ((), ())),
                            preferred_element_type=F32)
        if masked:
            qpos = lax.broadcasted_iota(jnp.int32, (tq, tq), 0)
            kpos = lax.broadcasted_iota(jnp.int32, (tq, tq), 1)
            s = jnp.where(kpos <= qpos, s, NEG_INF)
        m_old = m_scr[...]
        m_new = jnp.maximum(m_old, jnp.max(s, axis=-1, keepdims=True))
        p_scr[...] = jnp.exp2(s * c1 - m_new * c1).astype(BF16)
        alpha_scr[...] = jnp.exp2((m_old - m_new) * c1)
        m_scr[...] = m_new

    def body(ki, carry):
        tile(ki, False)
        return carry

    lax.fori_loop(0, qi, body, 0)
    tile(qi, True)
    flush(qi)
    o_ref[...] = _rms(acc_scr[:, :LANES] / acc_scr[:, LANES:], g_ref[0])


def _attn_prompt(q, k, v, cum_col, g_fox, batch, seq, tq):
    n, width = q.shape
    nh = width // LANES
    nt = seq // tq
    scale = float(LANES) ** -0.5
    return pl.pallas_call(
        functools.partial(_attn_prompt_kernel, scale=scale),
        grid=(batch, nh, nt),
        in_specs=[pl.BlockSpec((tq, LANES), lambda b, h, qi: (b * nt + qi, h)),
                  pl.BlockSpec((seq, LANES), lambda b, h, qi: (b, h)),
                  pl.BlockSpec((seq, LANES), lambda b, h, qi: (b, h)),
                  pl.BlockSpec((seq, nh), lambda b, h, qi: (b, 0)),
                  pl.BlockSpec((1, 1, LANES), lambda b, h, qi: (h, 0, 0))],
        out_specs=pl.BlockSpec((tq, LANES), lambda b, h, qi: (b * nt + qi, h)),
        out_shape=jax.ShapeDtypeStruct((n, width), F32),
        scratch_shapes=[pltpu.VMEM((tq, 2 * LANES), BF16), pltpu.VMEM((seq, 2 * LANES), BF16),
                        pltpu.VMEM((seq, 2 * LANES), BF16), pltpu.VMEM((tq, 1), F32),
                        pltpu.VMEM((tq, 2 * LANES), F32), pltpu.VMEM((tq, tq), BF16), pltpu.VMEM((tq, 1), F32)],
        compiler_params=_params(("parallel", "parallel", "arbitrary")),
        name="attn_prompt",
    )(q, k, v, cum_col, g_fox)


def _attn_sample_kernel(q_ref, kn_ref, vn_ref, kc_ref, vc_ref, cq_ref, ckc_ref, ckn_ref, g_ref, o_ref, *, scale):
    ts, nh = cq_ref.shape
    dn = (((1,), (1,)), ((), ()))
    qpos = lax.broadcasted_iota(jnp.int32, (ts, ts), 0)
    kpos = lax.broadcasted_iota(jnp.int32, (ts, ts), 1)
    for h in range(nh):
        sl = slice(h * LANES, (h + 1) * LANES)
        q = q_ref[:, sl].astype(BF16)
        cq = cq_ref[:, h:h + 1]
        s1 = lax.dot_general(q, kc_ref[:, h, :].astype(BF16), dn, preferred_element_type=F32) * scale
        s1 = s1 + (cq - ckc_ref[h])
        s2 = lax.dot_general(q, kn_ref[:, sl].astype(BF16), dn, preferred_element_type=F32) * scale
        s2 = jnp.where(kpos <= qpos, s2 + (cq - ckn_ref[h]), NEG_INF)
        m = jnp.maximum(jnp.max(s1, axis=-1, keepdims=True), jnp.max(s2, axis=-1, keepdims=True))
        p1 = jnp.exp(s1 - m)
        p2 = jnp.exp(s2 - m)
        l = jnp.sum(p1, axis=-1, keepdims=True) + jnp.sum(p2, axis=-1, keepdims=True)
        o = (jnp.dot(p1.astype(BF16), vc_ref[:, h, :].astype(BF16), preferred_element_type=F32)
             + jnp.dot(p2.astype(BF16), vn_ref[:, sl].astype(BF16), preferred_element_type=F32)) / l
        o_ref[:, sl] = _rms(o, g_ref[:, sl])


def _attn_sample(q, k_new, v_new, k_cache, v_cache, cq_col, ck_cache, ck_new, g_fox_row, ts):
    n, width = q.shape
    nb, past, nh, _ = k_cache.shape
    scale = float(LANES) ** -0.5
    tok = pl.BlockSpec((ts, width), lambda b: (b, 0))
    cache = pl.BlockSpec((None, past, nh, LANES), lambda b: (b, 0, 0, 0))
    return pl.pallas_call(
        functools.partial(_attn_sample_kernel, scale=scale),
        grid=(nb,),
        in_specs=[tok, tok, tok, cache, cache,
                  pl.BlockSpec((ts, nh), lambda b: (b, 0)),
                  pl.BlockSpec((nh, 1, past), lambda b: (b, 0, 0)),
                  pl.BlockSpec((nh, 1, ts), lambda b: (b, 0, 0)),
                  pl.BlockSpec((1, width), lambda b: (0, 0))],
        out_specs=tok,
        out_shape=jax.ShapeDtypeStruct((n, width), F32),
        compiler_params=_params(("parallel",)),
        name="attn_sample",
    )(q, k_new, v_new, k_cache, v_cache, cq_col, ck_cache, ck_new, g_fox_row)


def _merge_kernel(x_ref, fox_ref, us_ref, vs_ref, ws_ref, bs_ref, gs_ref, wo1_ref, wo2_ref, gf_ref,
                  o_ref, h_ref, cat_scr):
    tm = x_ref.shape[0]
    ng, chunk, _ = ws_ref.shape
    r = lax.broadcasted_iota(jnp.int32, (chunk, chunk), 0)
    c = lax.broadcasted_iota(jnp.int32, (chunk, chunk), 1)
    for g in range(ng):
        sl = slice(g * LANES, (g + 1) * LANES)
        wg = jnp.where(c <= r, ws_ref[g], 0.0).astype(BF16)
        bias = bs_ref[g]
        for j in range(tm // chunk):
            rows = slice(j * chunk, (j + 1) * chunk)
            mixed = jnp.dot(wg, vs_ref[rows, sl].astype(BF16), preferred_element_type=F32) + bias
            cat_scr[rows, sl] = _rms(us_ref[rows, sl] * mixed, gs_ref[:, sl]).astype(BF16)
    y = (x_ref[...]
         + jnp.dot(fox_ref[...].astype(BF16), wo1_ref[...], preferred_element_type=F32)
         + jnp.dot(cat_scr[...], wo2_ref[...], preferred_element_type=F32))
    o_ref[...] = y
    h_ref[...] = _rms(y, gf_ref[...])


def _merge(x2d, fox_n, us, vs, w_s, b_s_col, g_sgu_row, wo1, wo2, g_ffn_row, tm):
    n, d = x2d.shape
    width = fox_n.shape[1]
    ng, chunk, _ = w_s.shape
    full = lambda shape: pl.BlockSpec(shape, lambda i: (0,) * len(shape))
    tok = pl.BlockSpec((tm, width), lambda i: (i, 0))
    row = pl.BlockSpec((tm, d), lambda i: (i, 0))
    return pl.pallas_call(
        _merge_kernel,
        grid=(n // tm,),
        in_specs=[row, tok, tok, tok, full((ng, chunk, chunk)), full((ng, chunk, 1)), full((1, width)),
                  full((width, d)), full((width, d)), full((1, d))],
        out_specs=[row, row],
        out_shape=[jax.ShapeDtypeStruct((n, d), F32)] * 2,
        scratch_shapes=[pltpu.VMEM((tm, width), BF16)],
        compiler_params=_params(("parallel",)),
        name="merge",
    )(x2d, fox_n, us, vs, w_s, b_s_col, g_sgu_row, wo1, wo2, g_ffn_row)


def _top_rows(s, k, ids=None, payload=None):
    if ids is None:
        ids = lax.broadcasted_iota(jnp.int32, s.shape, 0).astype(F32)
    vals, picks = [], []
    for _ in range(k):
        m = jnp.max(s, axis=0, keepdims=True)
        i = jnp.min(jnp.where(s == m, ids, jnp.inf), axis=0, keepdims=True)
        hit = ids == i
        vals.append(m)
        picks.append(i if payload is None else jnp.max(jnp.where(hit, payload, -1.0), axis=0, keepdims=True))
        s = jnp.where(hit, -jnp.inf, s)
    return jnp.concatenate(vals, axis=0), jnp.concatenate(picks, axis=0)


def _stair_blocks(kk):
    need = {(a, j) for a in range(kk) for j in range(kk) if (a + 1) * (j + 1) <= kk}
    blocks = []
    for a in range(kk):
        nj = kk // (a + 1)
        if nj >= SUBLANES:
            for j0 in range(0, nj, SUBLANES):
                blocks.append(("row", a, j0, 0, min(SUBLANES, nj - j0)))
                need -= {(a, j0 + r) for r in range(SUBLANES)}
    for j in range(kk):
        for a0 in range(0, kk, SUBLANES):
            rs = [r for r in range(SUBLANES) if (a0 + r, j) in need]
            if rs:
                assert rs == list(range(rs[0], rs[-1] + 1))
                blocks.append(("col", a0, j, rs[0], rs[-1] + 1))
                need -= {(a0 + r, j) for r in rs}
    assert not need
    return blocks


def _pair_candidates(sv, si, n_keys):
    kk, cols = sv[0].shape
    sub = lax.broadcasted_iota(jnp.int32, (SUBLANES, cols), 0)
    vals, flat, expert = [], [], []
    for kind, a0, j0, lo, hi in _stair_blocks(kk):
        if kind == "row":
            v = sv[0][a0:a0 + 1] + sv[1][j0:j0 + SUBLANES]
            e = si[0][a0:a0 + 1] * n_keys + si[1][j0:j0 + SUBLANES]
            f = (a0 * kk + j0 + sub).astype(F32)
        else:
            v = sv[0][a0:a0 + SUBLANES] + sv[1][j0:j0 + 1]
            e = si[0][a0:a0 + SUBLANES] * n_keys + si[1][j0:j0 + 1]
            f = ((a0 + sub) * kk + j0).astype(F32)
        if (lo, hi) != (0, SUBLANES):
            v = jnp.where(sub < lo, -jnp.inf, jnp.where(sub < hi, v, -jnp.inf))
        vals.append(v)
        flat.append(f)
        expert.append(e)
    return jnp.concatenate(vals, axis=0), jnp.concatenate(flat, axis=0), jnp.concatenate(expert, axis=0)


def _route_kernel(h_ref, wq_ref, sk_ref, idx_ref, gate_ref, hb_scr):
    n_keys = sk_ref.shape[2]
    kk = PEER_TOPK

    @pl.when(pl.program_id(1) == 0)
    def _():
        hb_scr[...] = h_ref[...].astype(BF16)

    q = jnp.dot(hb_scr[...], wq_ref[...], preferred_element_type=F32).astype(BF16)
    dn = (((1,), (1,)), ((), ()))
    sv, si = [], []
    for c in range(2):
        st = lax.dot_general(sk_ref[0, c], q[:, c * LANES:(c + 1) * LANES], dn, preferred_element_type=F32)
        v, i = _top_rows(st, kk)
        sv.append(v)
        si.append(i)
    cand, flat, expert = _pair_candidates(sv, si, n_keys)
    top_s, top_i = _top_rows(cand, kk, ids=flat, payload=expert)
    e = jnp.exp(top_s - top_s[0:1])
    gate_ref[...] = e / jnp.sum(e, axis=0, keepdims=True)
    idx_ref[...] = top_i.astype(jnp.int32)


def _route(h2d, w_q, sub_keys, tm):
    n, d = h2d.shape
    n_heads, _, n_keys, half = sub_keys.shape
    assert n_keys == LANES and half == LANES
    sel = n_heads * PEER_TOPK
    return pl.pallas_call(
        _route_kernel,
        grid=(n // tm, n_heads),
        in_specs=[pl.BlockSpec((tm, d), lambda i, hd: (i, 0)),
                  pl.BlockSpec((d, 2 * half), lambda i, hd: (0, hd)),
                  pl.BlockSpec((1, 2, n_keys, half), lambda i, hd: (hd, 0, 0, 0))],
        out_specs=[pl.BlockSpec((PEER_TOPK, tm), lambda i, hd: (hd, i)),
                   pl.BlockSpec((PEER_TOPK, tm), lambda i, hd: (hd, i))],
        out_shape=[jax.ShapeDtypeStruct((sel, n), jnp.int32),
                   jax.ShapeDtypeStruct((sel, n), F32)],
        scratch_shapes=[pltpu.VMEM((tm, d), BF16)],
        compiler_params=_params(("parallel", "arbitrary")),
        name="route",
    )(h2d, w_q, sub_keys)


def _experts_kernel(idx_ref, h_ref, gate_ref, y_ref, uv_ref, o_ref, *scratch):
    tb, d = h_ref.shape
    sel = gate_ref.shape[0]
    bufs, sem = scratch[:-1], scratch[-1]
    nslot = len(bufs)
    ahead = nslot - 1

    def gather(t, slot):
        for j in range(sel):
            e = idx_ref[t * sel + j]
            pltpu.make_async_copy(uv_ref.at[e], bufs[slot].at[pl.ds(j, 1), :], sem.at[slot]).start(priority=j % 2)

    def wait(slot):
        pltpu.make_async_copy(bufs[slot], bufs[slot], sem.at[slot]).wait()

    for t0 in range(ahead):
        gather(t0, t0)
    lane = lax.broadcasted_iota(jnp.int32, gate_ref.shape, 1)
    nch = d // LANES

    def u_part(w):
        return lax.bitcast_convert_type(w << 16, F32)

    def v_part(w):
        return lax.bitcast_convert_type(w & jnp.int32(-65536), F32)

    def group(g, carry):
        base = pl.multiple_of(g * SUBLANES, SUBLANES)
        x8 = h_ref[pl.ds(base, SUBLANES), :]
        rows = []
        for i in range(SUBLANES):
            t = base + i
            slot = i % nslot
            buf = bufs[slot]
            gather(jnp.minimum(t + ahead, tb - 1), (i + ahead) % nslot)
            wait(slot)
            acc = jnp.zeros((sel, LANES), F32)
            for c in range(nch):
                sl = slice(c * LANES, (c + 1) * LANES)
                acc = acc + u_part(buf[:, sl]) * x8[i:i + 1, sl]
            dots = jnp.sum(acc, axis=-1, keepdims=True)
            gate = jnp.sum(jnp.where(lane == t, gate_ref[...], 0.0), axis=-1, keepdims=True)
            act = jnp.broadcast_to(jax.nn.gelu(dots) * gate, (sel, LANES))
            rows.append(jnp.concatenate(
                [jnp.sum(v_part(buf[:, c * LANES:(c + 1) * LANES]) * act, axis=0, keepdims=True)
                 for c in range(nch)], axis=1))
        o_ref[pl.ds(base, SUBLANES), :] = y_ref[pl.ds(base, SUBLANES), :] + jnp.concatenate(rows, axis=0)
        return carry

    lax.fori_loop(0, tb // SUBLANES, group, 0)
    for extra in range(ahead):
        wait((tb + extra) % nslot)


def _experts(idx_flat, h2d, gate_t, y2d, uv, tb):
    n, d = h2d.shape
    sel = gate_t.shape[0]
    assert tb % SUBLANES == 0 and SUBLANES % EXPERT_SLOTS == 0
    return pl.pallas_call(
        _experts_kernel,
        grid=(n // tb,),
        in_specs=[pl.BlockSpec((tb * sel,), lambda i: (i,), memory_space=pltpu.SMEM),
                  pl.BlockSpec((tb, d), lambda i: (i, 0)),
                  pl.BlockSpec((sel, tb), lambda i: (0, i)),
                  pl.BlockSpec((tb, d), lambda i: (i, 0)),
                  pl.BlockSpec(memory_space=pl.ANY)],
        out_specs=pl.BlockSpec((tb, d), lambda i: (i, 0)),
        out_shape=jax.ShapeDtypeStruct((n, d), F32),
        scratch_shapes=[pltpu.VMEM((sel, d), jnp.int32)] * EXPERT_SLOTS + [pltpu.SemaphoreType.DMA((EXPERT_SLOTS,))],
        compiler_params=_params(("arbitrary",)),
        name="experts",
    )(idx_flat, h2d, gate_t, y2d, uv)


def _pack_bf16_pair(lo, hi):
    def bits(x):
        return lax.bitcast_convert_type(lax.reduce_precision(x, exponent_bits=8, mantissa_bits=7), jnp.uint32)
    return lax.bitcast_convert_type((bits(lo) >> 16) | bits(hi), jnp.int32)


def _pick_tile(n, want):
    t = min(n, want)
    assert n % t == 0, (n, t)
    return t


def _layer_front(x2d, p):
    n, _ = x2d.shape
    tm = _pick_tile(n, 512)
    return _inproj(x2d, p["norm_mix_g"], p["w_main"], p["w_f_t"], p["b_f"], p["q_g"], p["k_g"], p["vs_g"], tm)


def _layer_back(x2d, fox_n, us, vs, p, chunk):
    n, _ = x2d.shape
    w_s = p["w_s"][:, :chunk, :chunk]
    b_s = p["b_s"][:, :chunk, None]
    y1, h2 = _merge(x2d, fox_n, us, vs, w_s, b_s, p["g_sgu"], p["wo1"], p["wo2"], p["norm_ffn_g"], _pick_tile(n, 256))
    idx_t, gate_t = _route(h2, p["w_q"], p["sub_keys"], _pick_tile(n, 256))
    idx_flat = idx_t.T.reshape(-1)
    return _experts(idx_flat, h2, gate_t, y1, p["uv"], _pick_tile(n, 512))


def kernel(x_prompt, x_sample, cache_fox_k, cache_fox_v, cache_fox_logf, norm_mix_g, w_in, b_f, q_norm_g, k_norm_g, sgu_v_norm_g, w_s, b_s, fox_out_norm_g, sgu_out_norm_g, w_out, norm_ffn_g, peer_w_q, peer_sub_keys, peer_u, peer_v):
    depth = w_in.shape[0]
    batch, seq, d = x_prompt.shape
    dec_batch, dec_seq, _ = x_sample.shape
    past = cache_fox_k.shape[2]
    nh, dh = cache_fox_k.shape[3], cache_fox_k.shape[4]
    ng, dg = sgu_v_norm_g.shape[1], sgu_v_norm_g.shape[2]
    assert dh == LANES and dg == LANES and nh == ng
    width = nh * dh
    assert w_in.shape[2] == 3 * width + nh + 2 * width

    y_p = x_prompt.reshape(batch * seq, d)
    y_s = x_sample.reshape(dec_batch * dec_seq, d)
    outs = [[] for _ in range(7)]
    for l in range(depth):
        wl = w_in[l]
        p = {
            "norm_mix_g": norm_mix_g[l][None, :],
            "w_main": jnp.concatenate([wl[:, :3 * width], wl[:, 3 * width + nh:]], axis=1).astype(BF16),
            "w_f_t": wl[:, 3 * width:3 * width + nh].T.astype(BF16),
            "b_f": b_f[l][:, None],
            "q_g": jnp.tile(q_norm_g[l], nh)[None, :],
            "k_g": jnp.tile(k_norm_g[l], nh)[None, :],
            "vs_g": sgu_v_norm_g[l].reshape(1, width),
            "w_s": w_s[l],
            "b_s": b_s[l],
            "g_sgu": sgu_out_norm_g[l].reshape(1, width),
            "wo1": w_out[l][:width].astype(BF16),
            "wo2": w_out[l][width:].astype(BF16),
            "norm_ffn_g": norm_ffn_g[l][None, :],
            "w_q": peer_w_q[l].astype(BF16),
            "sub_keys": peer_sub_keys[l].astype(BF16),
            "uv": _pack_bf16_pair(peer_u[l], peer_v[l])[:, None, :],
        }
        g_fox = fox_out_norm_g[l]

        q, k, v, us, vs, lf_t = _layer_front(y_s, p)
        lf_new = lf_t.reshape(nh, dec_batch, dec_seq).transpose(1, 0, 2)
        lf_all = jnp.concatenate([cache_fox_logf[l].astype(F32).transpose(0, 2, 1), lf_new], axis=2)
        total = past + dec_seq
        padded = -(-total // LANES) * LANES
        lf_all = jnp.pad(lf_all, ((0, 0), (0, 0), (0, padded - total))).reshape(dec_batch * nh, padded)
        cum = _cumsum_rows(lf_all).reshape(dec_batch, nh, padded)
        ck_cache = cum[:, :, :past].reshape(dec_batch * nh, 1, past)
        ck_new = cum[:, :, past:total]
        cq_col = ck_new.transpose(0, 2, 1).reshape(dec_batch * dec_seq, nh)
        fox_n = _attn_sample(q, k, v, cache_fox_k[l], cache_fox_v[l], cq_col, ck_cache,
                             ck_new.reshape(dec_batch * nh, 1, dec_seq), g_fox.reshape(1, width), dec_seq)
        y_s = _layer_back(y_s, fox_n, us, vs, p, dec_seq)
        outs[3].append(k.reshape(dec_batch, dec_seq, nh, dh))
        outs[4].append(v.reshape(dec_batch, dec_seq, nh, dh))
        outs[5].append(lf_t.T.reshape(dec_batch, dec_seq, nh))
        outs[6].append(vs.reshape(dec_batch, dec_seq, ng, dg))

        q, k, v, us, vs, lf_t = _layer_front(y_p, p)
        lf_rows = lf_t.reshape(nh, batch, seq).transpose(1, 0, 2).reshape(batch * nh, seq)
        cum = _cumsum_rows(lf_rows)
        cq_col = cum.reshape(batch, nh, seq).transpose(0, 2, 1).reshape(batch * seq, nh)
        fox_n = _attn_prompt(q, k, v, cq_col, g_fox[:, None, :], batch, seq, _pick_tile(seq, 512))
        y_p = _layer_back(y_p, fox_n, us, vs, p, SGU_CHUNK)
        outs[0].append(k.reshape(batch, seq, nh, dh))
        outs[1].append(v.reshape(batch, seq, nh, dh))
        outs[2].append(lf_t.T.reshape(batch, seq, nh))

    return (y_p.reshape(batch, seq, d), y_s.reshape(dec_batch, dec_seq, d)) + tuple(jnp.stack(o) for o in outs)
```
